```python
import math
import jax, jax.numpy as jnp
from jax import lax
import numpy as np

D_MODEL = 4096
BATCH = 16
SEQ = 256
DEPTH = 2
DEC_BATCH = 8
DEC_SEQ = 1024
PAST_LEN = 256

GRID_W = 64
F_WIDTH = D_MODEL // 4
F_GROUPS = 8
HY_WIDTH = D_MODEL // 4
HY_ORDER = 2
HY_CONV = 3
N_BANDS = 16
POS_EMB = 1 + 2 * N_BANDS
FILTER_HID = 64
DECAY_TARGET = 1e-2
SHORT_DECAY_FRAC = 0.3
LONG_DECAY_FRAC = 1.5
MIN_DECAY = math.log(DECAY_TARGET) / LONG_DECAY_FRAC
MAX_DECAY = math.log(DECAY_TARGET) / SHORT_DECAY_FRAC
HEAD_DIM = 128
N_Q_HEADS = (D_MODEL // 2) // HEAD_DIM
N_KV_HEADS = N_Q_HEADS // 4
GQA = N_Q_HEADS // N_KV_HEADS
ATT_WIDTH = N_Q_HEADS * HEAD_DIM
KV_WIDTH = N_KV_HEADS * HEAD_DIM
WINDOW = 128
BLOCK = 128
SCALE = HEAD_DIM ** -0.5
MASK_VALUE = -1e30
ROPE_THETA = 10000.0
ROPE_AXIS_DIM = HEAD_DIM // 2
D_FF = 256 * ((8 * D_MODEL // 3 + 255) // 256)
FFN_CONV = 3
LN_EPS = 1e-5
ALPHA = (2 * DEPTH) ** 0.25
BETA = (8 * DEPTH) ** -0.25
MOD_INIT = 0.5
IN_WIDTH = F_WIDTH + 3 * HY_WIDTH + ATT_WIDTH + 2 * KV_WIDTH
MIX_WIDTH = F_WIDTH + HY_WIDTH + ATT_WIDTH

kernel_name = "hybrid_fnet_hyena_swa_diffusion_step"


def layer_norm(x, g=None, b=None):
    xf = x.astype(jnp.float32)
    mu = jnp.mean(xf, axis=-1, keepdims=True)
    var = jnp.mean(jnp.square(xf - mu), axis=-1, keepdims=True)
    y = (xf - mu) * lax.rsqrt(var + LN_EPS)
    if g is not None:
        y = y * g + b
    return y.astype(x.dtype)


def modulate(h, shift, scale):
    return h * (1.0 + scale) + shift


def conv3(x, w, b):
    xp = jnp.pad(x, ((0, 0), (1, 1), (0, 0)))
    return xp[:, :-2] * w[0] + xp[:, 1:-1] * w[1] + xp[:, 2:] * w[2] + b


def fourier_mix(u):
    B, L, F = u.shape
    ug = u.reshape(B, L, F_GROUPS, F // F_GROUPS).astype(jnp.float32)
    return jnp.fft.fftn(ug, axes=(1, 3), norm="ortho").real.reshape(B, L, F).astype(u.dtype)


def implicit_filters(L, w1, b1, w2, b2, w3, freq):
    t = jnp.arange(L, dtype=jnp.float32)
    t_norm = t / max(L - 1, 1)
    w = 2.0 * math.pi * t / L
    bands = jnp.linspace(1e-4, N_BANDS - 1, N_BANDS, dtype=jnp.float32)
    feats = jnp.concatenate([t_norm[:, None], jnp.cos(w[:, None] * bands[None]),
                             jnp.sin(w[:, None] * bands[None])], axis=-1)
    h = jnp.sin(freq * (feats @ w1 + b1))
    h = jnp.sin(freq * (h @ w2 + b2))
    h = (h @ w3).astype(jnp.float32).reshape(L, HY_ORDER, 2, HY_WIDTH)
    deltas = jnp.abs(jnp.linspace(MIN_DECAY, MAX_DECAY, HY_WIDTH, dtype=jnp.float32))
    h = h * jnp.exp(-t_norm[:, None] * deltas[None, :])[:, None, None, :]
    fwd, bwd = h[:, :, 0], h[:, :, 1]
    circ = jnp.concatenate([fwd, jnp.zeros((1, HY_ORDER, HY_WIDTH), jnp.float32),
                            bwd[: L - 1][::-1]], axis=0)
    circ = circ / jnp.sum(jnp.abs(circ), axis=0, keepdims=True)
    return jnp.fft.rfft(circ, axis=0)


def hyena_mix(u, conv_w, conv_b, w1, b1, w2, b2, w3, freq, d_skip):
    B, L, _ = u.shape
    uc = conv3(u, conv_w, conv_b)
    v, x1, x2 = jnp.split(uc, 3, axis=-1)
    filt = implicit_filters(L, w1, b1, w2, b2, w3, freq)
    z = v.astype(jnp.float32)
    for o, gate in enumerate((x1, x2)):
        zf = jnp.fft.rfft(z, n=2 * L, axis=1)
        y = jnp.fft.irfft(zf * filt[None, :, o, :], n=2 * L, axis=1)[:, :L]
        z = gate.astype(jnp.float32) * (y + d_skip[o].astype(jnp.float32) * z)
    return z.astype(u.dtype)


def axial_rope(x, row, col):
    half = ROPE_AXIS_DIM // 2
    inv = ROPE_THETA ** (-jnp.arange(half, dtype=jnp.float32) * 2.0 / ROPE_AXIS_DIM)

    def rot(xa, pos):
        ang = pos.astype(jnp.float32)[:, None] * inv[None, :]
        cos, sin = jnp.cos(ang)[:, None, :], jnp.sin(ang)[:, None, :]
        x1, x2 = xa[..., :half], xa[..., half:]
        return jnp.concatenate([x1 * cos - x2 * sin, x2 * cos + x1 * sin], axis=-1)

    xf = x.astype(jnp.float32)
    return jnp.concatenate([rot(xf[..., :ROPE_AXIS_DIM], row),
                            rot(xf[..., ROPE_AXIS_DIM:], col)], axis=-1).astype(x.dtype)


def sink_softmax(s, sink):
    sk = sink.astype(jnp.float32)[:, :, None, None]
    m = jnp.maximum(jnp.max(s, axis=-1, keepdims=True), sk)
    p = jnp.exp(s - m)
    return p / (jnp.sum(p, axis=-1, keepdims=True) + jnp.exp(sk - m))


def context_attention(q, k, v, sink):
    B, S = q.shape[:2]
    nb = S // BLOCK
    qb = jnp.moveaxis(q.reshape(B, nb, BLOCK, N_KV_HEADS, GQA, HEAD_DIM), 1, 0)
    sk = sink.reshape(N_KV_HEADS, GQA)

    def one_block(qblk):
        s = jnp.einsum('bqhgd,bkhd->bhgqk', qblk, k, preferred_element_type=jnp.float32) * SCALE
        p = sink_softmax(s, sk)
        return jnp.einsum('bhgqk,bkhd->bqhgd', p.astype(v.dtype), v)

    out = lax.map(one_block, qb)
    return jnp.moveaxis(out, 0, 1).reshape(B, S, ATT_WIDTH)


def band_mask(nb):
    a = jnp.arange(BLOCK)[None, :, None]
    b = jnp.arange(3 * BLOCK)[None, None, :]
    n = jnp.arange(nb)[:, None, None]
    qpos = n * BLOCK + a
    kpos = (n - 1) * BLOCK + b
    return (jnp.abs(kpos - qpos) <= WINDOW) & (kpos >= 0) & (kpos < nb * BLOCK)


def latent_attention(q, k, v, ck, cv, sink):
    B, L = q.shape[:2]
    nb = L // BLOCK
    qb = q.reshape(B, nb, BLOCK, N_KV_HEADS, GQA, HEAD_DIM)

    def band(t):
        tp = jnp.pad(t.reshape(B, nb, BLOCK, N_KV_HEADS, HEAD_DIM),
                     ((0, 0), (1, 1), (0, 0), (0, 0), (0, 0)))
        return jnp.concatenate([tp[:, :-2], tp[:, 1:-1], tp[:, 2:]], axis=2)

    kb, vb = band(k), band(v)
    s_band = jnp.einsum('bnqhgd,bnkhd->bnhgqk', qb, kb, preferred_element_type=jnp.float32) * SCALE
    s_band = jnp.where(band_mask(nb)[None, :, None, None], s_band, MASK_VALUE)
    s_ctx = jnp.einsum('bnqhgd,bkhd->bnhgqk', qb, ck, preferred_element_type=jnp.float32) * SCALE
    p = sink_softmax(jnp.concatenate([s_band, s_ctx], axis=-1), sink.reshape(N_KV_HEADS, GQA))
    nk = 3 * BLOCK
    out = (jnp.einsum('bnhgqk,bnkhd->bnqhgd', p[..., :nk].astype(v.dtype), vb)
           + jnp.einsum('bnhgqk,bkhd->bnqhgd', p[..., nk:].astype(cv.dtype), cv))
    return out.reshape(B, L, ATT_WIDTH)


def conv_ffn(h, w_up, conv_w, conv_b, w_down):
    u = conv3(h @ w_up, conv_w, conv_b)
    g, val = jnp.split(u, 2, axis=-1)
    return (jax.nn.silu(g) * val) @ w_down


def setup_inputs(seed: int = 0) -> dict:
    key = jax.random.key(seed)
    ks = jax.random.split(key, 32)

    def nrm(k, shape, s):
        return jax.random.normal(k, shape, jnp.float32) * s

    return {
        "x_prompt": nrm(ks[0], (BATCH, SEQ, D_MODEL), 1.0),
        "x_sample": nrm(ks[1], (DEC_BATCH, DEC_SEQ, D_MODEL), 1.0),
        "cache_k": nrm(ks[2], (DEC_BATCH, DEPTH, PAST_LEN, N_KV_HEADS, HEAD_DIM), 1.0),
        "cache_v": nrm(ks[3], (DEC_BATCH, DEPTH, PAST_LEN, N_KV_HEADS, HEAD_DIM), 1.0),
        "c": nrm(ks[4], (DEC_BATCH, D_MODEL), 1.0),
        "c_ctx": nrm(ks[5], (D_MODEL,), 1.0),
        "w_mod": nrm(ks[6], (DEPTH, D_MODEL, 6 * D_MODEL), MOD_INIT * D_MODEL ** -0.5),
        "b_mod": nrm(ks[7], (DEPTH, 6 * D_MODEL), 0.02),
        "w_in": nrm(ks[8], (DEPTH, D_MODEL, IN_WIDTH), D_MODEL ** -0.5),
        "w_out": nrm(ks[9], (DEPTH, MIX_WIDTH, D_MODEL), BETA * MIX_WIDTH ** -0.5),
        "attn_sink": nrm(ks[10], (DEPTH, N_Q_HEADS), 0.5),
        "hy_conv_w": nrm(ks[11], (DEPTH, HY_CONV, 3 * HY_WIDTH), HY_CONV ** -0.5),
        "hy_conv_b": nrm(ks[12], (DEPTH, 3 * HY_WIDTH), 0.02),
        "flt_w1": nrm(ks[13], (DEPTH, POS_EMB, FILTER_HID), POS_EMB ** -0.5),
        "flt_b1": nrm(ks[14], (DEPTH, FILTER_HID), 0.1),
        "flt_w2": nrm(ks[15], (DEPTH, FILTER_HID, FILTER_HID), FILTER_HID ** -0.5),
        "flt_b2": nrm(ks[16], (DEPTH, FILTER_HID), 0.1),
        "flt_w3": nrm(ks[17], (DEPTH, FILTER_HID, HY_ORDER * 2 * HY_WIDTH), FILTER_HID ** -0.5),
        "flt_freq": 1.0 + nrm(ks[18], (DEPTH, FILTER_HID), 0.1),
        "hy_skip": nrm(ks[19], (DEPTH, HY_ORDER, HY_WIDTH), 1.0),
        "ln1_g": 1.0 + nrm(ks[20], (DEPTH, D_MODEL), 0.02),
        "ln1_b": nrm(ks[21], (DEPTH, D_MODEL), 0.02),
        "ffn_w_up": nrm(ks[22], (DEPTH, D_MODEL, 2 * D_FF), D_MODEL ** -0.5),
        "ffn_conv_w": nrm(ks[23], (DEPTH, FFN_CONV, 2 * D_FF), FFN_CONV ** -0.5),
        "ffn_conv_b": nrm(ks[24], (DEPTH, 2 * D_FF), 0.02),
        "ffn_w_down": nrm(ks[25], (DEPTH, D_FF, D_MODEL), BETA * D_FF ** -0.5),
        "ln2_g": 1.0 + nrm(ks[26], (DEPTH, D_MODEL), 0.02),
        "ln2_b": nrm(ks[27], (DEPTH, D_MODEL), 0.02),
    }


def reference(x_prompt, x_sample, cache_k, cache_v, c, c_ctx, w_mod, b_mod, w_in, w_out, attn_sink,
              hy_conv_w, hy_conv_b, flt_w1, flt_b1, flt_w2, flt_b2, flt_w3, flt_freq, hy_skip,
              ln1_g, ln1_b, ffn_w_up, ffn_conv_w, ffn_conv_b, ffn_w_down, ln2_g, ln2_b):
    l_lat = x_sample.shape[1]
    rows = l_lat // GRID_W
    row_pos = jnp.repeat(jnp.arange(rows), GRID_W)
    col_pos = jnp.arange(l_lat) % GRID_W
    splits = (F_WIDTH, F_WIDTH + 3 * HY_WIDTH, F_WIDTH + 3 * HY_WIDTH + ATT_WIDTH,
              IN_WIDTH - KV_WIDTH)

    def layer(x, mod, l, attend):
        B, L, _ = x.shape
        sh1, sc1, g1, sh2, sc2, g2 = jnp.split(mod, 6, axis=-1)
        h = modulate(layer_norm(x), sh1, sc1)
        u_f, u_h, q, k, v = jnp.split(h @ w_in[l], splits, axis=-1)
        q = q.reshape(B, L, N_Q_HEADS, HEAD_DIM)
        k = k.reshape(B, L, N_KV_HEADS, HEAD_DIM)
        v = v.reshape(B, L, N_KV_HEADS, HEAD_DIM)
        att = attend(q, k, v, attn_sink[l])
        hy = hyena_mix(u_h, hy_conv_w[l], hy_conv_b[l], flt_w1[l], flt_b1[l], flt_w2[l], flt_b2[l],
                       flt_w3[l], flt_freq[l], hy_skip[l])
        mix = jnp.concatenate([fourier_mix(u_f), hy, att], axis=-1)
        x = layer_norm(ALPHA * x + g1 * (mix @ w_out[l]), ln1_g[l], ln1_b[l])
        h = modulate(layer_norm(x), sh2, sc2)
        ff = conv_ffn(h, ffn_w_up[l], ffn_conv_w[l], ffn_conv_b[l], ffn_w_down[l])
        x = layer_norm(ALPHA * x + g2 * ff, ln2_g[l], ln2_b[l])
        return x, k, v

    y_prompt = x_prompt
    y_sample = x_sample
    ctx_k, ctx_v = [], []
    for l in range(DEPTH):
        mod_ctx = (jax.nn.silu(c_ctx) @ w_mod[l] + b_mod[l])[None, None, :]
        y_prompt, k_l, v_l = layer(y_prompt, mod_ctx, l, context_attention)
        ctx_k.append(k_l)
        ctx_v.append(v_l)

        mod_lat = (jax.nn.silu(c) @ w_mod[l] + b_mod[l])[:, None, :]

        def attend_latent(q, k, v, s, l=l):
            return latent_attention(axial_rope(q, row_pos, col_pos), axial_rope(k, row_pos, col_pos),
                                    v, cache_k[:, l], cache_v[:, l], s)

        y_sample, _, _ = layer(y_sample, mod_lat, l, attend_latent)

    new_cache_k = jnp.stack(ctx_k, axis=1)
    new_cache_v = jnp.stack(ctx_v, axis=1)
    return (y_prompt, y_sample, new_cache_k, new_cache_v)
```

```python
import functools
import math

import ml_dtypes
import numpy as np
import jax
import jax.numpy as jnp
from jax import lax
from jax.experimental import pallas as pl
from jax.experimental.pallas import tpu as pltpu

F32 = jnp.float32
BF16 = jnp.bfloat16

D_MODEL = 4096
DEPTH = 2
GRID_W = 64
F_WIDTH = D_MODEL // 4
F_GROUPS = 8
F_GROUP_W = F_WIDTH // F_GROUPS
HY_WIDTH = D_MODEL // 4
HY_ORDER = 2
N_BANDS = 16
POS_EMB = 1 + 2 * N_BANDS
FILTER_HID = 64
DECAY_TARGET = 1e-2
MIN_DECAY = math.log(DECAY_TARGET) / 1.5
MAX_DECAY = math.log(DECAY_TARGET) / 0.3
HEAD_DIM = 128
N_Q_HEADS = (D_MODEL // 2) // HEAD_DIM
N_KV_HEADS = N_Q_HEADS // 4
GQA = N_Q_HEADS // N_KV_HEADS
ATT_WIDTH = N_Q_HEADS * HEAD_DIM
KV_WIDTH = N_KV_HEADS * HEAD_DIM
WINDOW = 128
BLOCK = 128
SCALE = HEAD_DIM ** -0.5
ROPE_THETA = 10000.0
ROPE_AXIS_DIM = HEAD_DIM // 2
D_FF = 256 * ((8 * D_MODEL // 3 + 255) // 256)
LN_EPS = 1e-5
ALPHA = (2 * DEPTH) ** 0.25
IN_WIDTH = F_WIDTH + 3 * HY_WIDTH + ATT_WIDTH + 2 * KV_WIDTH
MIX_WIDTH = F_WIDTH + HY_WIDTH + ATT_WIDTH

HY_OFF = F_WIDTH
Q_OFF = F_WIDTH + 3 * HY_WIDTH
K_OFF = Q_OFF + ATT_WIDTH
V_OFF = K_OFF + KV_WIDTH

MOD_ROWS = 16
LANE = 128
SUBLANE = 8
VMEM_LIMIT = 56 * 1024 * 1024

TM_IN = 512
TN_IN = 1024
TM_OUT = 256
TN_OUT = 512
TM_FFN = 512
TF_FFN = 256
HY_CT = 256
TN_MOD = 512
TK_MOD = 512
ROW_CHUNK = 32


def _cparams(sem):
    return pltpu.CompilerParams(dimension_semantics=sem, vmem_limit_bytes=VMEM_LIMIT)


def _const_spec(shape):
    return pl.BlockSpec(shape, lambda *_: (0,) * len(shape), pipeline_mode=pl.Buffered(1))


def _bdot(a, b):
    return jnp.dot(a, b, preferred_element_type=F32)


def _ln_rows(x):
    mu = jnp.mean(x, axis=-1, keepdims=True)
    xc = x - mu
    var = jnp.mean(xc * xc, axis=-1, keepdims=True)
    return xc * lax.rsqrt(var + LN_EPS)


def _silu(x):
    return x / (1.0 + jnp.exp(-x))


def _for_row_chunks(n_rows, body):
    def step(c, carry):
        body(pl.ds(pl.multiple_of(c * ROW_CHUNK, ROW_CHUNK), ROW_CHUNK))
        return carry
    lax.fori_loop(0, n_rows // ROW_CHUNK, step, 0)


def _hi_lo(x):
    hi = x.astype(ml_dtypes.bfloat16)
    lo = (x - hi.astype(np.float64)).astype(ml_dtypes.bfloat16)
    return jnp.asarray(hi), jnp.asarray(lo)


@functools.lru_cache(maxsize=None)
def _dft_mats(L):
    n = np.arange(L)
    ang = np.pi * ((n[:, None] * n[None, :]) % (2 * L)) / L
    fwd = np.concatenate([np.cos(ang), -np.sin(ang)], axis=0)
    wgt = np.full(L, 2.0)
    wgt[0] = 1.0
    inv = (fwd * np.concatenate([wgt, wgt])[:, None]).T / (2 * L)
    ang2 = 2.0 * np.pi * ((n[:, None] * n[None, :]) % L) / L
    fm = np.concatenate([np.cos(ang2), -np.sin(ang2)], axis=1) / np.sqrt(L)
    return _hi_lo(fwd), _hi_lo(inv), _hi_lo(fm)


@functools.lru_cache(maxsize=None)
def _chan_dft():
    n = np.arange(F_GROUP_W)
    ang = 2.0 * np.pi * ((n[:, None] * n[None, :]) % F_GROUP_W) / F_GROUP_W
    return _hi_lo(np.concatenate([np.cos(ang), np.sin(ang)], axis=1) / np.sqrt(F_GROUP_W))


def _mod_kernel(c_ref, w_ref, b_ref, o_ref):
    acc = b_ref[0] + jnp.zeros((MOD_ROWS, TN_MOD), F32)
    for k in range(0, D_MODEL, TK_MOD):
        s = _silu(c_ref[:, k:k + TK_MOD]).astype(BF16)
        acc = acc + _bdot(s, w_ref[0, k:k + TK_MOD, :].astype(BF16))
    o_ref[0] = acc


def _mod_call(cvec, w_mod, b_mod):
    n_out = w_mod.shape[-1]
    return pl.pallas_call(
        _mod_kernel,
        grid=(DEPTH, n_out // TN_MOD),
        in_specs=[
            pl.BlockSpec((MOD_ROWS, D_MODEL), lambda l, j: (0, 0)),
            pl.BlockSpec((1, D_MODEL, TN_MOD), lambda l, j: (l, 0, j)),
            pl.BlockSpec((1, 1, TN_MOD), lambda l, j: (l, 0, j)),
        ],
        out_specs=pl.BlockSpec((1, MOD_ROWS, TN_MOD), lambda l, j: (l, 0, j)),
        out_shape=jax.ShapeDtypeStruct((DEPTH, MOD_ROWS, n_out), F32),
        compiler_params=_cparams(("arbitrary", "arbitrary")),
        name="adaln_mod",
    )(cvec, w_mod, b_mod.reshape(DEPTH, 1, n_out))


def _inproj_kernel(x_ref, mod_ref, w_ref, o_ref, h_ref):
    @pl.when(pl.program_id(1) == 0)
    def _():
        def body(rows):
            h = _ln_rows(x_ref[rows, :]) * (1.0 + mod_ref[0, 1:2, :]) + mod_ref[0, 0:1, :]
            h_ref[rows, :] = h.astype(BF16)
        _for_row_chunks(TM_IN, body)

    o_ref[...] = _bdot(h_ref[...], w_ref[...])


def _inproj_call(x, mod, mod_of_tile, w_in):
    T = x.shape[0]
    return pl.pallas_call(
        _inproj_kernel,
        grid=(T // TM_IN, IN_WIDTH // TN_IN),
        in_specs=[
            pl.BlockSpec((TM_IN, D_MODEL), lambda i, j: (i, 0)),
            pl.BlockSpec((1, 6, D_MODEL), lambda i, j: (mod_of_tile(i, TM_IN), 0, 0)),
            pl.BlockSpec((D_MODEL, TN_IN), lambda i, j: (0, j)),
        ],
        out_specs=pl.BlockSpec((TM_IN, TN_IN), lambda i, j: (i, j)),
        out_shape=jax.ShapeDtypeStruct((T, IN_WIDTH), F32),
        scratch_shapes=[pltpu.VMEM((TM_IN, D_MODEL), BF16)],
        compiler_params=_cparams(("arbitrary", "arbitrary")),
        name="inproj",
    )(x, mod, w_in)


def _fourier_kernel(u_ref, cc_hi_ref, cc_lo_ref, fm_hi_ref, fm_lo_ref, o_ref, p_ref):
    L = u_ref.shape[1]
    rc = min(L, 256)
    for g in range(F_GROUPS):
        cols = slice(g * F_GROUP_W, (g + 1) * F_GROUP_W)
        for r in range(0, L, rc):
            ug = u_ref[0, r:r + rc, cols].astype(BF16)
            a = _bdot(ug, cc_hi_ref[...]) + _bdot(ug, cc_lo_ref[...])
            p_ref[r:r + rc, cols] = a[:, :F_GROUP_W].astype(BF16)
            p_ref[L + r:L + r + rc, cols] = a[:, F_GROUP_W:].astype(BF16)
    for r in range(0, L, rc):
        y = _bdot(fm_hi_ref[r:r + rc, :], p_ref[...]) + _bdot(fm_lo_ref[r:r + rc, :], p_ref[...])
        o_ref[0, r:r + rc, :] = y.astype(BF16)


def _fourier_call(u3, L):
    B = u3.shape[0]
    _, _, (fm_hi, fm_lo) = _dft_mats(L)
    cc_hi, cc_lo = _chan_dft()
    return pl.pallas_call(
        _fourier_kernel,
        grid=(B,),
        in_specs=[
            pl.BlockSpec((1, L, F_WIDTH), lambda b: (b, 0, 0)),
            _const_spec((F_GROUP_W, 2 * F_GROUP_W)),
            _const_spec((F_GROUP_W, 2 * F_GROUP_W)),
            _const_spec((L, 2 * L)),
            _const_spec((L, 2 * L)),
        ],
        out_specs=pl.BlockSpec((1, L, F_WIDTH), lambda b: (b, 0, 0)),
        out_shape=jax.ShapeDtypeStruct((B, L, F_WIDTH), BF16),
        scratch_shapes=[pltpu.VMEM((2 * L, F_WIDTH), BF16)],
        compiler_params=_cparams(("arbitrary",)),
        name="fourier_mix",
    )(u3, cc_hi, cc_lo, fm_hi, fm_lo)


def _filter_taps_kernel(feat_ref, w1_ref, b1_ref, w2_ref, b2_ref, freq_ref, w3f_ref, w3b_ref, decay_ref,
                        fwd_ref, bsh_ref):
    L = feat_ref.shape[0]
    hp = lax.Precision.HIGHEST
    feats = feat_ref[...]
    freq = freq_ref[...]
    h = jnp.sin(freq * (jnp.dot(feats, w1_ref[...], precision=hp, preferred_element_type=F32) + b1_ref[...]))
    h = jnp.sin(freq * (jnp.dot(h, w2_ref[...], precision=hp, preferred_element_type=F32) + b2_ref[...]))
    win = jnp.exp(-feats[:, 0:1] * decay_ref[...])
    fwd = jnp.dot(h, w3f_ref[...], precision=hp, preferred_element_type=F32) * win
    bwd = jnp.dot(h, w3b_ref[...], precision=hp, preferred_element_type=F32) * win
    row = lax.broadcasted_iota(jnp.int32, bwd.shape, 0)
    bsh = jnp.where(row == 0, 0.0, pltpu.roll(bwd, 1, axis=0))
    norm = jnp.sum(jnp.abs(fwd), axis=0, keepdims=True) + jnp.sum(jnp.abs(bsh), axis=0, keepdims=True)
    fwd_ref[...] = fwd / norm
    bsh_ref[...] = bsh / norm


def _filter_spectrum_kernel(fwd_ref, bsh_ref, f_hi_ref, f_lo_ref, hr_ref, hi_ref, hn_ref):
    L = fwd_ref.shape[0]
    fwd = fwd_ref[...]
    bsh = bsh_ref[...]
    both = jnp.concatenate([fwd, bsh], axis=1)
    b_hi = both.astype(BF16)
    b_lo = (both - b_hi.astype(F32)).astype(BF16)
    ct = fwd.shape[1]
    rc = min(L, 256)
    for r in range(0, L, rc):
        def spec(rows):
            return (_bdot(f_hi_ref[rows, :], b_hi) + _bdot(f_lo_ref[rows, :], b_hi)
                    + _bdot(f_hi_ref[rows, :], b_lo))
        c = spec(slice(r, r + rc))
        s = spec(slice(L + r, L + r + rc))
        hr_ref[r:r + rc, :] = c[:, :ct] + c[:, ct:]
        hi_ref[r:r + rc, :] = s[:, :ct] - s[:, ct:]
    row = lax.broadcasted_iota(jnp.int32, fwd.shape, 0)
    alt = jnp.where((row & 1) == 0, 1.0, -1.0)
    hn_ref[...] = jnp.sum(alt * (fwd + bsh), axis=0, keepdims=True)


def _filters_call(L, w1, b1, w2, b2, w3, freq):
    t = np.arange(L, dtype=np.float32)
    t_norm = t / np.float32(max(L - 1, 1))
    w = np.float32(2.0 * math.pi) * t / np.float32(L)
    bands = np.linspace(1e-4, N_BANDS - 1, N_BANDS, dtype=np.float32)
    feats = np.zeros((L, LANE), np.float32)
    feats[:, 0] = t_norm
    feats[:, 1:1 + N_BANDS] = np.cos(w[:, None] * bands[None])
    feats[:, 1 + N_BANDS:POS_EMB] = np.sin(w[:, None] * bands[None])
    decay = np.abs(np.linspace(MIN_DECAY, MAX_DECAY, HY_WIDTH, dtype=np.float32))[None, :]
    w1p = jnp.zeros((LANE, FILTER_HID), F32).at[:POS_EMB].set(w1)
    nct = HY_WIDTH // HY_CT
    ocw = HY_ORDER * HY_WIDTH
    row = lambda v: v.reshape(1, -1)
    small = lambda shape: pl.BlockSpec(shape, lambda o, ct: (0, 0))
    fwd, bsh = pl.pallas_call(
        _filter_taps_kernel,
        grid=(HY_ORDER, nct),
        in_specs=[
            small((L, LANE)), small((LANE, FILTER_HID)), small((1, FILTER_HID)),
            small((FILTER_HID, FILTER_HID)), small((1, FILTER_HID)), small((1, FILTER_HID)),
            pl.BlockSpec((FILTER_HID, HY_CT), lambda o, ct: (0, o * 2 * nct + ct)),
            pl.BlockSpec((FILTER_HID, HY_CT), lambda o, ct: (0, o * 2 * nct + nct + ct)),
            pl.BlockSpec((1, HY_CT), lambda o, ct: (0, ct)),
        ],
        out_specs=[pl.BlockSpec((L, HY_CT), lambda o, ct: (0, o * nct + ct))] * 2,
        out_shape=[jax.ShapeDtypeStruct((L, ocw), F32)] * 2,
        compiler_params=_cparams(("arbitrary", "arbitrary")),
        name="hyena_filter_taps",
    )(jnp.asarray(feats), w1p, row(b1), w2, row(b2), row(freq), w3, w3, jnp.asarray(decay))
    (f_hi, f_lo), _, _ = _dft_mats(L)
    return pl.pallas_call(
        _filter_spectrum_kernel,
        grid=(ocw // HY_CT,),
        in_specs=[
            pl.BlockSpec((L, HY_CT), lambda j: (0, j)),
            pl.BlockSpec((L, HY_CT), lambda j: (0, j)),
            _const_spec((2 * L, L)),
            _const_spec((2 * L, L)),
        ],
        out_specs=[pl.BlockSpec((L, HY_CT), lambda j: (0, j)),
                   pl.BlockSpec((L, HY_CT), lambda j: (0, j)),
                   pl.BlockSpec((1, HY_CT), lambda j: (0, j))],
        out_shape=[jax.ShapeDtypeStruct((L, ocw), F32), jax.ShapeDtypeStruct((L, ocw), F32),
                   jax.ShapeDtypeStruct((1, ocw), F32)],
        compiler_params=_cparams(("arbitrary",)),
        name="hyena_filter_spectrum",
    )(fwd, bsh, f_hi, f_lo)


def _conv3_rows(x, w_ref, b_ref, first, last):
    n = x.shape[0]
    prev = jnp.where(first, 0.0, pltpu.roll(x, 1, axis=0))
    nxt = jnp.where(last, 0.0, pltpu.roll(x, n - 1, axis=0))
    return prev * w_ref[0:1, :] + x * w_ref[1:2, :] + nxt * w_ref[2:3, :] + b_ref[...]


def _hyena_kernel(v_ref, x1_ref, x2_ref, wv_ref, w1_ref, w2_ref, bv_ref, b1_ref, b2_ref,
                  hr0_ref, hi0_ref, hr1_ref, hi1_ref, hn0_ref, hn1_ref, skip_ref,
                  f_hi_ref, f_lo_ref, i_hi_ref, i_lo_ref, o_ref, spec_ref):
    L = v_ref.shape[1]
    rc = min(L, 256)
    row = lax.broadcasted_iota(jnp.int32, (L, HY_CT), 0)
    first = row == 0
    last = row == L - 1
    alt = jnp.where((row & 1) == 0, 1.0, -1.0)
    z = _conv3_rows(v_ref[0], wv_ref, bv_ref, first, last)
    gates = (_conv3_rows(x1_ref[0], w1_ref, b1_ref, first, last),
             _conv3_rows(x2_ref[0], w2_ref, b2_ref, first, last))
    filt = ((hr0_ref, hi0_ref, hn0_ref), (hr1_ref, hi1_ref, hn1_ref))
    for o in range(HY_ORDER):
        hr_ref, hi_ref, hn_ref = filt[o]
        zb = z.astype(BF16)
        for r in range(0, L, rc):
            zr = _bdot(f_hi_ref[r:r + rc, :], zb) + _bdot(f_lo_ref[r:r + rc, :], zb)
            zi = _bdot(f_hi_ref[L + r:L + r + rc, :], zb) + _bdot(f_lo_ref[L + r:L + r + rc, :], zb)
            hr = hr_ref[r:r + rc, :]
            hi = hi_ref[r:r + rc, :]
            spec_ref[r:r + rc, :] = (zr * hr - zi * hi).astype(BF16)
            spec_ref[L + r:L + r + rc, :] = (zr * hi + zi * hr).astype(BF16)
        nyq = jnp.sum(alt * z, axis=0, keepdims=True) * hn_ref[...] * (1.0 / (2 * L))
        ys = []
        for r in range(0, L, rc):
            ys.append(_bdot(i_hi_ref[r:r + rc, :], spec_ref[...]) + _bdot(i_lo_ref[r:r + rc, :], spec_ref[...]))
        y = jnp.concatenate(ys, axis=0) + alt * nyq
        z = gates[o] * (y + skip_ref[o:o + 1, :] * z)
    o_ref[0] = z.astype(BF16)


def _hyena_call(u3, L, conv_w, conv_b, spectrum, skip):
    B = u3.shape[0]
    nct = HY_WIDTH // HY_CT
    hr, hi, hn = spectrum
    (f_hi, f_lo), (i_hi, i_lo), _ = _dft_mats(L)
    off = HY_OFF // HY_CT
    u_spec = lambda part: pl.BlockSpec((1, L, HY_CT), lambda b, ct: (b, 0, off + part * nct + ct))
    cw_spec = lambda part: pl.BlockSpec((3, HY_CT), lambda b, ct: (0, part * nct + ct))
    cb_spec = lambda part: pl.BlockSpec((1, HY_CT), lambda b, ct: (0, part * nct + ct))
    h_spec = lambda o: pl.BlockSpec((L, HY_CT), lambda b, ct: (0, o * nct + ct))
    n_spec = lambda o: pl.BlockSpec((1, HY_CT), lambda b, ct: (0, o * nct + ct))
    cb = conv_b.reshape(1, -1)
    return pl.pallas_call(
        _hyena_kernel,
        grid=(B, nct),
        in_specs=[
            u_spec(0), u_spec(1), u_spec(2),
            cw_spec(0), cw_spec(1), cw_spec(2),
            cb_spec(0), cb_spec(1), cb_spec(2),
            h_spec(0), h_spec(0), h_spec(1), h_spec(1), n_spec(0), n_spec(1),
            pl.BlockSpec((HY_ORDER, HY_CT), lambda b, ct: (0, ct)),
            _const_spec((2 * L, L)), _const_spec((2 * L, L)),
            _const_spec((L, 2 * L)), _const_spec((L, 2 * L)),
        ],
        out_specs=pl.BlockSpec((1, L, HY_CT), lambda b, ct: (b, 0, ct)),
        out_shape=jax.ShapeDtypeStruct((B, L, HY_WIDTH), BF16),
        scratch_shapes=[pltpu.VMEM((2 * L, HY_CT), BF16)],
        compiler_params=_cparams(("arbitrary", "arbitrary")),
        name="hyena_mix",
    )(u3, u3, u3, conv_w, conv_w, conv_w, cb, cb, cb, hr, hi, hr, hi, hn, hn, skip,
      f_hi, f_lo, i_hi, i_lo)


def _sink_softmax_pv(scores, values, sink):
    m = sink
    for s in scores:
        m = jnp.maximum(m, jnp.max(s, axis=-1, keepdims=True))
    ps = [jnp.exp(s - m) for s in scores]
    den = jnp.exp(sink - m)
    for p in ps:
        den = den + jnp.sum(p, axis=-1, keepdims=True)
    inv = 1.0 / den
    out = None
    for p, v in zip(ps, values):
        t = _bdot((p * inv).astype(BF16), v)
        out = t if out is None else out + t
    return out


def _ctx_attn_kernel(sink_ref, q_ref, k_ref, v_ref, o_ref):
    h = pl.program_id(1)
    k = k_ref[0].astype(BF16)
    v = v_ref[0].astype(BF16)
    for g in range(GQA):
        cols = slice(g * HEAD_DIM, (g + 1) * HEAD_DIM)
        q = q_ref[0, :, cols].astype(BF16)
        s = lax.dot_general(q, k, (((1,), (1,)), ((), ())), preferred_element_type=F32) * SCALE
        o = _sink_softmax_pv([s], [v], sink_ref[h * GQA + g])
        o_ref[0, :, cols] = o.astype(BF16)


def _ctx_attn_call(u3, sink):
    B, S, _ = u3.shape
    gw = GQA * HEAD_DIM
    return pl.pallas_call(
        _ctx_attn_kernel,
        grid=(B, N_KV_HEADS),
        in_specs=[
            pl.BlockSpec(memory_space=pltpu.SMEM),
            pl.BlockSpec((1, S, gw), lambda b, h: (b, 0, Q_OFF // gw + h)),
            pl.BlockSpec((1, S, HEAD_DIM), lambda b, h: (b, 0, K_OFF // HEAD_DIM + h)),
            pl.BlockSpec((1, S, HEAD_DIM), lambda b, h: (b, 0, V_OFF // HEAD_DIM + h)),
        ],
        out_specs=pl.BlockSpec((1, S, gw), lambda b, h: (b, 0, h)),
        out_shape=jax.ShapeDtypeStruct((B, S, ATT_WIDTH), BF16),
        compiler_params=_cparams(("arbitrary", "arbitrary")),
        name="context_attention",
    )(sink, u3, u3, u3)


def _rope(x, cos, sin_signed, low_half):
    half = ROPE_AXIS_DIM // 2
    partner = jnp.where(low_half, pltpu.roll(x, HEAD_DIM - half, axis=1), pltpu.roll(x, half, axis=1))
    return x * cos + partner * sin_signed


def _lat_attn_kernel(sink_ref, q_ref, k_ref, v_ref, ck_ref, cv_ref, cos_ref, sin_ref, o_ref, kr_ref, vb_ref):
    h = pl.program_id(1)
    L = k_ref.shape[1]
    nb = L // BLOCK
    def low_half(rows):
        lane = lax.broadcasted_iota(jnp.int32, (rows, HEAD_DIM), 1)
        return (lane & (ROPE_AXIS_DIM - 1)) < (ROPE_AXIS_DIM // 2)

    kr_ref[...] = _rope(k_ref[0], cos_ref[...], sin_ref[...], low_half(L)).astype(BF16)
    vb_ref[...] = v_ref[0].astype(BF16)
    ck = ck_ref[0, 0].astype(BF16)
    cv = cv_ref[0, 0].astype(BF16)
    nt = (((1,), (1,)), ((), ()))
    for g in range(GQA):
        cols = slice(g * HEAD_DIM, (g + 1) * HEAD_DIM)
        sink = sink_ref[h * GQA + g]
        for n in range(nb):
            rows = slice(n * BLOCK, (n + 1) * BLOCK)
            q = _rope(q_ref[0, rows, cols], cos_ref[rows, :], sin_ref[rows, :], low_half(BLOCK)).astype(BF16)
            lo = max(n - 1, 0) * BLOCK
            hi = min(n + 2, nb) * BLOCK
            s_lat = lax.dot_general(q, kr_ref[lo:hi, :], nt, preferred_element_type=F32) * SCALE
            qpos = n * BLOCK + lax.broadcasted_iota(jnp.int32, s_lat.shape, 0)
            kpos = lo + lax.broadcasted_iota(jnp.int32, s_lat.shape, 1)
            s_lat = jnp.where(jnp.abs(kpos - qpos) <= WINDOW, s_lat, -1e30)
            s_ctx = lax.dot_general(q, ck, nt, preferred_element_type=F32) * SCALE
            o = _sink_softmax_pv([s_lat, s_ctx], [vb_ref[lo:hi, :], cv], sink)
            o_ref[0, rows, cols] = o.astype(BF16)


def _lat_attn_call(u3, cache_k_l, cache_v_l, layer, sink, cos, sin_signed):
    B, L, _ = u3.shape
    P = cache_k_l.shape[2]
    gw = GQA * HEAD_DIM
    c_spec = pl.BlockSpec((1, 1, P, HEAD_DIM), lambda b, h: (b, layer, 0, h))
    return pl.pallas_call(
        _lat_attn_kernel,
        grid=(B, N_KV_HEADS),
        in_specs=[
            pl.BlockSpec(memory_space=pltpu.SMEM),
            pl.BlockSpec((1, L, gw), lambda b, h: (b, 0, Q_OFF // gw + h)),
            pl.BlockSpec((1, L, HEAD_DIM), lambda b, h: (b, 0, K_OFF // HEAD_DIM + h)),
            pl.BlockSpec((1, L, HEAD_DIM), lambda b, h: (b, 0, V_OFF // HEAD_DIM + h)),
            c_spec, c_spec,
            _const_spec((L, HEAD_DIM)), _const_spec((L, HEAD_DIM)),
        ],
        out_specs=pl.BlockSpec((1, L, gw), lambda b, h: (b, 0, h)),
        out_shape=jax.ShapeDtypeStruct((B, L, ATT_WIDTH), BF16),
        scratch_shapes=[pltpu.VMEM((L, HEAD_DIM), BF16), pltpu.VMEM((L, HEAD_DIM), BF16)],
        compiler_params=_cparams(("arbitrary", "arbitrary")),
        name="latent_attention",
    )(sink, u3, u3, u3, cache_k_l, cache_v_l, cos, sin_signed)


def _rope_tables(L):
    rows = L // GRID_W
    row_pos = jnp.repeat(jnp.arange(rows), GRID_W)
    col_pos = jnp.arange(L) % GRID_W
    half = ROPE_AXIS_DIM // 2
    inv = ROPE_THETA ** (-jnp.arange(half, dtype=F32) * 2.0 / ROPE_AXIS_DIM)
    cos_parts, sin_parts = [], []
    for pos in (row_pos, col_pos):
        ang = pos.astype(F32)[:, None] * inv[None, :]
        cos_parts += [jnp.cos(ang), jnp.cos(ang)]
        sin_parts += [-jnp.sin(ang), jnp.sin(ang)]
    return jnp.concatenate(cos_parts, axis=-1), jnp.concatenate(sin_parts, axis=-1)


def _outproj_kernel(f_ref, hy_ref, att_ref, wf_ref, wh_ref, wa_ref, x_ref, mod_ref, g_ref, b_ref, o_ref, acc_ref):
    j = pl.program_id(1)
    acc_ref[j] = (_bdot(f_ref[...], wf_ref[...]) + _bdot(hy_ref[...], wh_ref[...])
                  + _bdot(att_ref[...], wa_ref[...]))

    @pl.when(j == pl.num_programs(1) - 1)
    def _():
        def body(rows):
            mix = jnp.concatenate([acc_ref[jj, rows, :] for jj in range(D_MODEL // TN_OUT)], axis=1)
            y = ALPHA * x_ref[rows, :] + mod_ref[0, 2:3, :] * mix
            o_ref[rows, :] = _ln_rows(y) * g_ref[...] + b_ref[...]
        _for_row_chunks(TM_OUT, body)


def _outproj_call(f, hy, att, w_out, x, mod, mod_of_tile, ln_g, ln_b):
    T = x.shape[0]
    assert F_WIDTH == HY_WIDTH and ATT_WIDTH == F_WIDTH + HY_WIDTH
    return pl.pallas_call(
        _outproj_kernel,
        grid=(T // TM_OUT, D_MODEL // TN_OUT),
        in_specs=[
            pl.BlockSpec((TM_OUT, F_WIDTH), lambda i, j: (i, 0)),
            pl.BlockSpec((TM_OUT, HY_WIDTH), lambda i, j: (i, 0)),
            pl.BlockSpec((TM_OUT, ATT_WIDTH), lambda i, j: (i, 0)),
            pl.BlockSpec((F_WIDTH, TN_OUT), lambda i, j: (0, j)),
            pl.BlockSpec((HY_WIDTH, TN_OUT), lambda i, j: (1, j)),
            pl.BlockSpec((ATT_WIDTH, TN_OUT), lambda i, j: (1, j)),
            pl.BlockSpec((TM_OUT, D_MODEL), lambda i, j: (i, 0)),
            pl.BlockSpec((1, 6, D_MODEL), lambda i, j: (mod_of_tile(i, TM_OUT), 0, 0)),
            pl.BlockSpec((1, D_MODEL), lambda i, j: (0, 0)),
            pl.BlockSpec((1, D_MODEL), lambda i, j: (0, 0)),
        ],
        out_specs=pl.BlockSpec((TM_OUT, D_MODEL), lambda i, j: (i, 0)),
        out_shape=jax.ShapeDtypeStruct((T, D_MODEL), F32),
        scratch_shapes=[pltpu.VMEM((D_MODEL // TN_OUT, TM_OUT, TN_OUT), F32)],
        compiler_params=_cparams(("arbitrary", "arbitrary")),
        name="outproj_ln",
    )(f, hy, att, w_out, w_out, w_out, x, mod, ln_g.reshape(1, -1), ln_b.reshape(1, -1))


def _ffn_kernel(seq_len, x_ref, xp_ref, xn_ref, mod_ref, wg_ref, wv_ref, cwg_ref, cwv_ref, cbg_ref, cbv_ref,
                wd_ref, g_ref, b_ref, o_ref, h_ref, hh_ref):
    j = pl.program_id(1)
    tm = x_ref.shape[0]
    shift = mod_ref[0, 3:4, :]
    scale = 1.0 + mod_ref[0, 4:5, :]

    @pl.when(j == 0)
    def _():
        def body(rows):
            h_ref[rows, :] = (_ln_rows(x_ref[rows, :]) * scale + shift).astype(BF16)
        _for_row_chunks(tm, body)
        halo = jnp.concatenate([xp_ref[...], xn_ref[...]], axis=0)
        hh_ref[...] = (_ln_rows(halo) * scale + shift).astype(BF16)

    row = lax.broadcasted_iota(jnp.int32, (tm, TF_FFN), 0)
    pos = (pl.program_id(0) * tm + row) & (seq_len - 1)
    seq_first = pos == 0
    seq_last = pos == seq_len - 1
    h = h_ref[...]
    hh = hh_ref[...]

    def up_conv(w_ref, cw_ref, cb_ref):
        u = _bdot(h, w_ref[...])
        uh = _bdot(hh, w_ref[...])
        prev = jnp.where(row == 0, uh[SUBLANE - 1:SUBLANE, :], pltpu.roll(u, 1, axis=0))
        nxt = jnp.where(row == tm - 1, uh[SUBLANE:SUBLANE + 1, :], pltpu.roll(u, tm - 1, axis=0))
        prev = jnp.where(seq_first, 0.0, prev)
        nxt = jnp.where(seq_last, 0.0, nxt)
        return prev * cw_ref[0:1, :] + u * cw_ref[1:2, :] + nxt * cw_ref[2:3, :] + cb_ref[...]

    gate = up_conv(wg_ref, cwg_ref, cbg_ref)
    val = up_conv(wv_ref, cwv_ref, cbv_ref)
    act = (_silu(gate) * val).astype(BF16)
    part = _bdot(act, wd_ref[...])

    @pl.when(j == 0)
    def _():
        o_ref[...] = part

    @pl.when(j > 0)
    def _():
        o_ref[...] += part

    @pl.when(j == pl.num_programs(1) - 1)
    def _():
        def body(rows):
            y = ALPHA * x_ref[rows, :] + mod_ref[0, 5:6, :] * o_ref[rows, :]
            o_ref[rows, :] = _ln_rows(y) * g_ref[...] + b_ref[...]
        _for_row_chunks(tm, body)


def _ffn_call(x, seq_len, mod, mod_of_tile, w_up, conv_w, conv_b, w_down, ln_g, ln_b):
    T = x.shape[0]
    nf = D_FF // TF_FFN
    halo_blocks = TM_FFN // SUBLANE
    n_halo = T // SUBLANE
    cb = conv_b.reshape(1, -1)
    return pl.pallas_call(
        functools.partial(_ffn_kernel, seq_len),
        grid=(T // TM_FFN, nf),
        in_specs=[
            pl.BlockSpec((TM_FFN, D_MODEL), lambda i, j: (i, 0), pipeline_mode=pl.Buffered(1)),
            pl.BlockSpec((SUBLANE, D_MODEL), lambda i, j: (jnp.maximum(i * halo_blocks - 1, 0), 0)),
            pl.BlockSpec((SUBLANE, D_MODEL), lambda i, j: (jnp.minimum((i + 1) * halo_blocks, n_halo - 1), 0)),
            pl.BlockSpec((1, 6, D_MODEL), lambda i, j: (mod_of_tile(i, TM_FFN), 0, 0)),
            pl.BlockSpec((D_MODEL, TF_FFN), lambda i, j: (0, j)),
            pl.BlockSpec((D_MODEL, TF_FFN), lambda i, j: (0, nf + j)),
            pl.BlockSpec((3, TF_FFN), lambda i, j: (0, j)),
            pl.BlockSpec((3, TF_FFN), lambda i, j: (0, nf + j)),
            pl.BlockSpec((1, TF_FFN), lambda i, j: (0, j)),
            pl.BlockSpec((1, TF_FFN), lambda i, j: (0, nf + j)),
            pl.BlockSpec((TF_FFN, D_MODEL), lambda i, j: (j, 0)),
            pl.BlockSpec((1, D_MODEL), lambda i, j: (0, 0)),
            pl.BlockSpec((1, D_MODEL), lambda i, j: (0, 0)),
        ],
        out_specs=pl.BlockSpec((TM_FFN, D_MODEL), lambda i, j: (i, 0)),
        out_shape=jax.ShapeDtypeStruct((T, D_MODEL), F32),
        scratch_shapes=[pltpu.VMEM((TM_FFN, D_MODEL), BF16), pltpu.VMEM((2 * SUBLANE, D_MODEL), BF16)],
        compiler_params=_cparams(("arbitrary", "arbitrary")),
        name="conv_ffn_ln",
    )(x, x, x, mod, w_up, w_up, conv_w, conv_w, cb, cb, w_down, ln_g.reshape(1, -1), ln_b.reshape(1, -1))


def _layer(x, B, L, mod, mod_base, mod_per_seq, p, attend):
    def mod_of_tile(i, tm):
        return mod_base + (i * tm // L) * mod_per_seq

    u = _inproj_call(x, mod, mod_of_tile, p["w_in"])
    u3 = u.reshape(B, L, IN_WIDTH)
    f = _fourier_call(u3, L)
    spectrum = _filters_call(L, p["flt_w1"], p["flt_b1"], p["flt_w2"], p["flt_b2"], p["flt_w3"], p["flt_freq"])
    hy = _hyena_call(u3, L, p["hy_conv_w"], p["hy_conv_b"], spectrum, p["hy_skip"])
    att = attend(u3)
    T = B * L
    x = _outproj_call(f.reshape(T, F_WIDTH), hy.reshape(T, HY_WIDTH), att.reshape(T, ATT_WIDTH), p["w_out"],
                      x, mod, mod_of_tile, p["ln1_g"], p["ln1_b"])
    x = _ffn_call(x, L, mod, mod_of_tile, p["ffn_w_up"], p["ffn_conv_w"], p["ffn_conv_b"], p["ffn_w_down"],
                  p["ln2_g"], p["ln2_b"])
    return x, u3


def kernel(x_prompt, x_sample, cache_k, cache_v, c, c_ctx, w_mod, b_mod, w_in, w_out, attn_sink,
           hy_conv_w, hy_conv_b, flt_w1, flt_b1, flt_w2, flt_b2, flt_w3, flt_freq, hy_skip,
           ln1_g, ln1_b, ffn_w_up, ffn_conv_w, ffn_conv_b, ffn_w_down, ln2_g, ln2_b):
    Bp, S, _ = x_prompt.shape
    Bs, Ls, _ = x_sample.shape
    past = cache_k.shape[2]

    cvec = jnp.zeros((MOD_ROWS, D_MODEL), F32).at[0].set(c_ctx).at[1:1 + Bs].set(c)
    mods = _mod_call(cvec, w_mod, b_mod).reshape(DEPTH, MOD_ROWS, 6, D_MODEL)

    w_in_b, w_out_b = w_in.astype(BF16), w_out.astype(BF16)
    w_up_b, w_down_b = ffn_w_up.astype(BF16), ffn_w_down.astype(BF16)
    ck = cache_k.reshape(Bs, DEPTH, past, KV_WIDTH)
    cv = cache_v.reshape(Bs, DEPTH, past, KV_WIDTH)
    cos, sin_signed = _rope_tables(Ls)

    xp = x_prompt.reshape(Bp * S, D_MODEL)
    xs = x_sample.reshape(Bs * Ls, D_MODEL)
    ctx_k, ctx_v = [], []
    for l in range(DEPTH):
        p = dict(w_in=w_in_b[l], w_out=w_out_b[l], flt_w1=flt_w1[l], flt_b1=flt_b1[l], flt_w2=flt_w2[l],
                 flt_b2=flt_b2[l], flt_w3=flt_w3[l], flt_freq=flt_freq[l], hy_conv_w=hy_conv_w[l],
                 hy_conv_b=hy_conv_b[l], hy_skip=hy_skip[l], ln1_g=ln1_g[l], ln1_b=ln1_b[l],
                 ffn_w_up=w_up_b[l], ffn_conv_w=ffn_conv_w[l], ffn_conv_b=ffn_conv_b[l],
                 ffn_w_down=w_down_b[l], ln2_g=ln2_g[l], ln2_b=ln2_b[l])
        sink = attn_sink[l]
        xp, up3 = _layer(xp, Bp, S, mods[l], 0, 0, p, lambda u3: _ctx_attn_call(u3, sink))
        ctx_k.append(up3[:, :, K_OFF:K_OFF + KV_WIDTH].reshape(Bp, S, N_KV_HEADS, HEAD_DIM))
        ctx_v.append(up3[:, :, V_OFF:V_OFF + KV_WIDTH].reshape(Bp, S, N_KV_HEADS, HEAD_DIM))
        xs, _ = _layer(xs, Bs, Ls, mods[l], 1, 1, p,
                       lambda u3: _lat_attn_call(u3, ck, cv, l, sink, cos, sin_signed))

    return (xp.reshape(Bp, S, D_MODEL), xs.reshape(Bs, Ls, D_MODEL),
            jnp.stack(ctx_k, axis=1), jnp.stack(ctx_v, axis=1))
```

```python
import functools
import math

import ml_dtypes
import numpy as np
import jax
import jax.numpy as jnp
from jax import lax
from jax.experimental import pallas as pl
from jax.experimental.pallas import tpu as pltpu

F32 = jnp.float32
BF16 = jnp.bfloat16

D_MODEL = 4096
DEPTH = 2
GRID_W = 64
F_WIDTH = D_MODEL // 4
F_GROUPS = 8
F_GROUP_W = F_WIDTH // F_GROUPS
HY_WIDTH = D_MODEL // 4
HY_ORDER = 2
N_BANDS = 16
POS_EMB = 1 + 2 * N_BANDS
FILTER_HID = 64
DECAY_TARGET = 1e-2
MIN_DECAY = math.log(DECAY_TARGET) / 1.5
MAX_DECAY = math.log(DECAY_TARGET) / 0.3
HEAD_DIM = 128
N_Q_HEADS = (D_MODEL // 2) // HEAD_DIM
N_KV_HEADS = N_Q_HEADS // 4
GQA = N_Q_HEADS // N_KV_HEADS
ATT_WIDTH = N_Q_HEADS * HEAD_DIM
KV_WIDTH = N_KV_HEADS * HEAD_DIM
WINDOW = 128
BLOCK = 128
SCALE = HEAD_DIM ** -0.5
ROPE_THETA = 10000.0
ROPE_AXIS_DIM = HEAD_DIM // 2
D_FF = 256 * ((8 * D_MODEL // 3 + 255) // 256)
LN_EPS = 1e-5
ALPHA = (2 * DEPTH) ** 0.25
IN_WIDTH = F_WIDTH + 3 * HY_WIDTH + ATT_WIDTH + 2 * KV_WIDTH
MIX_WIDTH = F_WIDTH + HY_WIDTH + ATT_WIDTH

HY_OFF = F_WIDTH
Q_OFF = F_WIDTH + 3 * HY_WIDTH
K_OFF = Q_OFF + ATT_WIDTH
V_OFF = K_OFF + KV_WIDTH

MOD_ROWS = 16
LANE = 128
SUBLANE = 8
VMEM_LIMIT = 56 * 1024 * 1024

BF16_ROWS = 16
TM_MM = 1024
TN_MM = 512
TM_LN = 256
TM_FFN = 512
TF_FFN = 256
FFN_PIECES = 8
HY_CT = 256
TN_MOD = 512
TK_MOD = 512
ROW_CHUNK = 32


def _cparams(sem):
    return pltpu.CompilerParams(dimension_semantics=sem, vmem_limit_bytes=VMEM_LIMIT)


def _const_spec(shape):
    return pl.BlockSpec(shape, lambda *_: (0,) * len(shape), pipeline_mode=pl.Buffered(1))


def _bdot(a, b):
    return jnp.dot(a, b, preferred_element_type=F32)


def _ln_rows(x):
    mu = jnp.mean(x, axis=-1, keepdims=True)
    xc = x - mu
    var = jnp.mean(xc * xc, axis=-1, keepdims=True)
    return xc * lax.rsqrt(var + LN_EPS)


def _silu(x):
    return x / (1.0 + jnp.exp(-x))


def _for_row_chunks(n_rows, body):
    def step(c, carry):
        body(pl.ds(pl.multiple_of(c * ROW_CHUNK, ROW_CHUNK), ROW_CHUNK))
        return carry
    lax.fori_loop(0, n_rows // ROW_CHUNK, step, 0)


def _hi_lo(x):
    hi = x.astype(ml_dtypes.bfloat16)
    lo = (x - hi.astype(np.float64)).astype(ml_dtypes.bfloat16)
    return jnp.asarray(hi), jnp.asarray(lo)


@functools.lru_cache(maxsize=None)
def _dft_mats(L):
    n = np.arange(L)
    ang = np.pi * ((n[:, None] * n[None, :]) % (2 * L)) / L
    fwd = np.concatenate([np.cos(ang), -np.sin(ang)], axis=0)
    wgt = np.full(L, 2.0)
    wgt[0] = 1.0
    inv = (fwd * np.concatenate([wgt, wgt])[:, None]).T / (2 * L)
    ang2 = 2.0 * np.pi * ((n[:, None] * n[None, :]) % L) / L
    fm = np.concatenate([np.cos(ang2), -np.sin(ang2)], axis=1) / np.sqrt(L)
    return _hi_lo(fwd), _hi_lo(inv), _hi_lo(fm)


@functools.lru_cache(maxsize=None)
def _chan_dft():
    n = np.arange(F_GROUP_W)
    ang = 2.0 * np.pi * ((n[:, None] * n[None, :]) % F_GROUP_W) / F_GROUP_W
    return _hi_lo(np.concatenate([np.cos(ang), np.sin(ang)], axis=1) / np.sqrt(F_GROUP_W))


def _mod_kernel(c_ref, w_ref, b_ref, o_ref):
    acc = b_ref[0] + jnp.zeros((MOD_ROWS, TN_MOD), F32)
    for k in range(0, D_MODEL, TK_MOD):
        s = _silu(c_ref[:, k:k + TK_MOD]).astype(BF16)
        acc = acc + _bdot(s, w_ref[0, k:k + TK_MOD, :].astype(BF16))
    o_ref[0] = acc


def _mod_call(cvec, w_mod, b_mod):
    n_out = w_mod.shape[-1]
    return pl.pallas_call(
        _mod_kernel,
        grid=(DEPTH, n_out // TN_MOD),
        in_specs=[
            pl.BlockSpec((MOD_ROWS, D_MODEL), lambda l, j: (0, 0)),
            pl.BlockSpec((1, D_MODEL, TN_MOD), lambda l, j: (l, 0, j)),
            pl.BlockSpec((1, 1, TN_MOD), lambda l, j: (l, 0, j)),
        ],
        out_specs=pl.BlockSpec((1, MOD_ROWS, TN_MOD), lambda l, j: (l, 0, j)),
        out_shape=jax.ShapeDtypeStruct((DEPTH, MOD_ROWS, n_out), F32),
        compiler_params=_cparams(("arbitrary", "arbitrary")),
        name="adaln_mod",
    )(cvec, w_mod, b_mod.reshape(DEPTH, 1, n_out))


def _inproj_kernel(h_ref, w_ref, o_ref):
    o_ref[...] = _bdot(h_ref[...], w_ref[...])


def _inproj_call(h, w_in, layer):
    T = h.shape[0]
    return pl.pallas_call(
        _inproj_kernel,
        grid=(T // TM_MM, IN_WIDTH // TN_MM),
        in_specs=[
            pl.BlockSpec((TM_MM, D_MODEL), lambda i, j: (i, 0)),
            pl.BlockSpec((None, D_MODEL, TN_MM), lambda i, j: (layer, 0, j)),
        ],
        out_specs=pl.BlockSpec((TM_MM, TN_MM), lambda i, j: (i, j)),
        out_shape=jax.ShapeDtypeStruct((T, IN_WIDTH), F32),
        compiler_params=_cparams(("arbitrary", "arbitrary")),
        name="inproj",
    )(h, w_in)


def _mod_spec(layer, modrow_of_tile, tm):
    return pl.BlockSpec((None, None, 6, D_MODEL), lambda i: (layer, modrow_of_tile(i, tm), 0, 0))


def _ln_mod_kernel(x_ref, mod_ref, h_ref):
    def body(rows):
        h = _ln_rows(x_ref[rows, :]) * (1.0 + mod_ref[1:2, :]) + mod_ref[0:1, :]
        h_ref[rows, :] = h.astype(BF16)
    _for_row_chunks(TM_LN, body)


def _ln_mod_call(x, mods, layer, modrow_of_tile):
    T = x.shape[0]
    return pl.pallas_call(
        _ln_mod_kernel,
        grid=(T // TM_LN,),
        in_specs=[pl.BlockSpec((TM_LN, D_MODEL), lambda i: (i, 0)), _mod_spec(layer, modrow_of_tile, TM_LN)],
        out_specs=pl.BlockSpec((TM_LN, D_MODEL), lambda i: (i, 0)),
        out_shape=jax.ShapeDtypeStruct((T, D_MODEL), BF16),
        compiler_params=_cparams(("arbitrary",)),
        name="ln_modulate",
    )(x, mods)


def _resid_ln_kernel(gate_row, h_row, x_ref, d_ref, modg_ref, g_ref, b_ref, *rest):
    if h_row is None:
        (xo_ref,) = rest
    else:
        modh_ref, xo_ref, ho_ref = rest

    def body(rows):
        y = ALPHA * x_ref[rows, :] + modg_ref[gate_row:gate_row + 1, :] * d_ref[rows, :]
        xn = _ln_rows(y) * g_ref[...] + b_ref[...]
        xo_ref[rows, :] = xn
        if h_row is not None:
            h = _ln_rows(xn) * (1.0 + modh_ref[h_row + 1:h_row + 2, :]) + modh_ref[h_row:h_row + 1, :]
            ho_ref[rows, :] = h.astype(BF16)
    _for_row_chunks(TM_LN, body)


def _resid_ln_call(x, d, mods, layer, gate_row, ln_g, ln_b, modrow_of_tile, h_layer=None, h_row=None):
    T = x.shape[0]
    tile = pl.BlockSpec((TM_LN, D_MODEL), lambda i: (i, 0))
    vec = pl.BlockSpec((None, 1, D_MODEL), lambda i: (layer, 0, 0))
    in_specs = [tile, tile, _mod_spec(layer, modrow_of_tile, TM_LN), vec, vec]
    args = [x, d, mods, ln_g.reshape(DEPTH, 1, D_MODEL), ln_b.reshape(DEPTH, 1, D_MODEL)]
    out_specs = [tile]
    out_shape = [jax.ShapeDtypeStruct((T, D_MODEL), F32)]
    if h_row is not None:
        in_specs.append(_mod_spec(h_layer, modrow_of_tile, TM_LN))
        args.append(mods)
        out_specs.append(tile)
        out_shape.append(jax.ShapeDtypeStruct((T, D_MODEL), BF16))
    out = pl.pallas_call(
        functools.partial(_resid_ln_kernel, gate_row, h_row),
        grid=(T // TM_LN,),
        in_specs=in_specs,
        out_specs=out_specs,
        out_shape=out_shape,
        compiler_params=_cparams(("arbitrary",)),
        name="residual_ln",
    )(*args)
    return out if h_row is not None else (out[0], None)


def _fourier_kernel(u_ref, cc_hi_ref, cc_lo_ref, fm_hi_ref, fm_lo_ref, o_ref, p_ref):
    L = u_ref.shape[1]
    rc = min(L, 256)
    for g in range(F_GROUPS):
        cols = slice(g * F_GROUP_W, (g + 1) * F_GROUP_W)
        for r in range(0, L, rc):
            ug = u_ref[0, r:r + rc, cols].astype(BF16)
            a = _bdot(ug, cc_hi_ref[...]) + _bdot(ug, cc_lo_ref[...])
            p_ref[r:r + rc, cols] = a[:, :F_GROUP_W].astype(BF16)
            p_ref[L + r:L + r + rc, cols] = a[:, F_GROUP_W:].astype(BF16)
    for r in range(0, L, rc):
        y = _bdot(fm_hi_ref[r:r + rc, :], p_ref[...]) + _bdot(fm_lo_ref[r:r + rc, :], p_ref[...])
        o_ref[0, r:r + rc, :] = y.astype(BF16)


def _fourier_call(u3, L):
    B = u3.shape[0]
    _, _, (fm_hi, fm_lo) = _dft_mats(L)
    cc_hi, cc_lo = _chan_dft()
    return pl.pallas_call(
        _fourier_kernel,
        grid=(B,),
        in_specs=[
            pl.BlockSpec((1, L, F_WIDTH), lambda b: (b, 0, 0)),
            _const_spec((F_GROUP_W, 2 * F_GROUP_W)),
            _const_spec((F_GROUP_W, 2 * F_GROUP_W)),
            _const_spec((L, 2 * L)),
            _const_spec((L, 2 * L)),
        ],
        out_specs=pl.BlockSpec((1, L, F_WIDTH), lambda b: (b, 0, 0)),
        out_shape=jax.ShapeDtypeStruct((B, L, F_WIDTH), BF16),
        scratch_shapes=[pltpu.VMEM((2 * L, F_WIDTH), BF16)],
        compiler_params=_cparams(("arbitrary",)),
        name="fourier_mix",
    )(u3, cc_hi, cc_lo, fm_hi, fm_lo)


def _filter_taps_kernel(feat_ref, w1_ref, b1_ref, w2_ref, b2_ref, freq_ref, w3f_ref, w3b_ref, decay_ref,
                        fwd_ref, bsh_ref):
    L = feat_ref.shape[0]
    hp = lax.Precision.HIGHEST
    feats = feat_ref[...]
    freq = freq_ref[...]
    h = jnp.sin(freq * (jnp.dot(feats, w1_ref[...], precision=hp, preferred_element_type=F32) + b1_ref[...]))
    h = jnp.sin(freq * (jnp.dot(h, w2_ref[...], precision=hp, preferred_element_type=F32) + b2_ref[...]))
    win = jnp.exp(-feats[:, 0:1] * decay_ref[...])
    fwd = jnp.dot(h, w3f_ref[...], precision=hp, preferred_element_type=F32) * win
    bwd = jnp.dot(h, w3b_ref[...], precision=hp, preferred_element_type=F32) * win
    row = lax.broadcasted_iota(jnp.int32, bwd.shape, 0)
    bsh = jnp.where(row == 0, 0.0, pltpu.roll(bwd, 1, axis=0))
    norm = jnp.sum(jnp.abs(fwd), axis=0, keepdims=True) + jnp.sum(jnp.abs(bsh), axis=0, keepdims=True)
    fwd_ref[...] = fwd / norm
    bsh_ref[...] = bsh / norm


def _filter_spectrum_kernel(fwd_ref, bsh_ref, f_hi_ref, f_lo_ref, hr_ref, hi_ref, hn_ref):
    L = fwd_ref.shape[0]
    fwd = fwd_ref[...]
    bsh = bsh_ref[...]
    both = jnp.concatenate([fwd, bsh], axis=1)
    b_hi = both.astype(BF16)
    b_lo = (both - b_hi.astype(F32)).astype(BF16)
    ct = fwd.shape[1]
    rc = min(L, 256)
    for r in range(0, L, rc):
        def spec(rows):
            return (_bdot(f_hi_ref[rows, :], b_hi) + _bdot(f_lo_ref[rows, :], b_hi)
                    + _bdot(f_hi_ref[rows, :], b_lo))
        c = spec(slice(r, r + rc))
        s = spec(slice(L + r, L + r + rc))
        hr_ref[r:r + rc, :] = c[:, :ct] + c[:, ct:]
        hi_ref[r:r + rc, :] = s[:, :ct] - s[:, ct:]
    row = lax.broadcasted_iota(jnp.int32, fwd.shape, 0)
    alt = jnp.where((row & 1) == 0, 1.0, -1.0)
    hn_ref[...] = jnp.sum(alt * (fwd + bsh), axis=0, keepdims=True)


def _filters_call(L, w1, b1, w2, b2, w3, freq):
    t = np.arange(L, dtype=np.float32)
    t_norm = t / np.float32(max(L - 1, 1))
    w = np.float32(2.0 * math.pi) * t / np.float32(L)
    bands = np.linspace(1e-4, N_BANDS - 1, N_BANDS, dtype=np.float32)
    feats = np.zeros((L, LANE), np.float32)
    feats[:, 0] = t_norm
    feats[:, 1:1 + N_BANDS] = np.cos(w[:, None] * bands[None])
    feats[:, 1 + N_BANDS:POS_EMB] = np.sin(w[:, None] * bands[None])
    decay = np.abs(np.linspace(MIN_DECAY, MAX_DECAY, HY_WIDTH, dtype=np.float32))[None, :]
    w1p = jnp.zeros((LANE, FILTER_HID), F32).at[:POS_EMB].set(w1)
    nct = HY_WIDTH // HY_CT
    ocw = HY_ORDER * HY_WIDTH
    row = lambda v: v.reshape(1, -1)
    small = lambda shape: pl.BlockSpec(shape, lambda o, ct: (0, 0))
    fwd, bsh = pl.pallas_call(
        _filter_taps_kernel,
        grid=(HY_ORDER, nct),
        in_specs=[
            small((L, LANE)), small((LANE, FILTER_HID)), small((1, FILTER_HID)),
            small((FILTER_HID, FILTER_HID)), small((1, FILTER_HID)), small((1, FILTER_HID)),
            pl.BlockSpec((FILTER_HID, HY_CT), lambda o, ct: (0, o * 2 * nct + ct)),
            pl.BlockSpec((FILTER_HID, HY_CT), lambda o, ct: (0, o * 2 * nct + nct + ct)),
            pl.BlockSpec((1, HY_CT), lambda o, ct: (0, ct)),
        ],
        out_specs=[pl.BlockSpec((L, HY_CT), lambda o, ct: (0, o * nct + ct))] * 2,
        out_shape=[jax.ShapeDtypeStruct((L, ocw), F32)] * 2,
        compiler_params=_cparams(("arbitrary", "arbitrary")),
        name="hyena_filter_taps",
    )(jnp.asarray(feats), w1p, row(b1), w2, row(b2), row(freq), w3, w3, jnp.asarray(decay))
    (f_hi, f_lo), _, _ = _dft_mats(L)
    return pl.pallas_call(
        _filter_spectrum_kernel,
        grid=(ocw // HY_CT,),
        in_specs=[
            pl.BlockSpec((L, HY_CT), lambda j: (0, j)),
            pl.BlockSpec((L, HY_CT), lambda j: (0, j)),
            _const_spec((2 * L, L)),
            _const_spec((2 * L, L)),
        ],
        out_specs=[pl.BlockSpec((L, HY_CT), lambda j: (0, j)),
                   pl.BlockSpec((L, HY_CT), lambda j: (0, j)),
                   pl.BlockSpec((1, HY_CT), lambda j: (0, j))],
        out_shape=[jax.ShapeDtypeStruct((L, ocw), F32), jax.ShapeDtypeStruct((L, ocw), F32),
                   jax.ShapeDtypeStruct((1, ocw), F32)],
        compiler_params=_cparams(("arbitrary",)),
        name="hyena_filter_spectrum",
    )(fwd, bsh, f_hi, f_lo)


def _conv3_rows(x, w_ref, b_ref, first, last):
    n = x.shape[0]
    prev = jnp.where(first, 0.0, pltpu.roll(x, 1, axis=0))
    nxt = jnp.where(last, 0.0, pltpu.roll(x, n - 1, axis=0))
    return prev * w_ref[0:1, :] + x * w_ref[1:2, :] + nxt * w_ref[2:3, :] + b_ref[...]


def _hyena_kernel(v_ref, x1_ref, x2_ref, wv_ref, w1_ref, w2_ref, bv_ref, b1_ref, b2_ref,
                  hr0_ref, hi0_ref, hr1_ref, hi1_ref, hn0_ref, hn1_ref, skip_ref,
                  f_hi_ref, f_lo_ref, i_hi_ref, i_lo_ref, o_ref, spec_ref):
    L = v_ref.shape[1]
    rc = min(L, 256)
    row = lax.broadcasted_iota(jnp.int32, (L, HY_CT), 0)
    first = row == 0
    last = row == L - 1
    alt = jnp.where((row & 1) == 0, 1.0, -1.0)
    z = _conv3_rows(v_ref[0], wv_ref, bv_ref, first, last)
    gates = (_conv3_rows(x1_ref[0], w1_ref, b1_ref, first, last),
             _conv3_rows(x2_ref[0], w2_ref, b2_ref, first, last))
    filt = ((hr0_ref, hi0_ref, hn0_ref), (hr1_ref, hi1_ref, hn1_ref))
    for o in range(HY_ORDER):
        hr_ref, hi_ref, hn_ref = filt[o]
        zb = z.astype(BF16)
        for r in range(0, L, rc):
            zr = _bdot(f_hi_ref[r:r + rc, :], zb) + _bdot(f_lo_ref[r:r + rc, :], zb)
            zi = _bdot(f_hi_ref[L + r:L + r + rc, :], zb) + _bdot(f_lo_ref[L + r:L + r + rc, :], zb)
            hr = hr_ref[r:r + rc, :]
            hi = hi_ref[r:r + rc, :]
            spec_ref[r:r + rc, :] = (zr * hr - zi * hi).astype(BF16)
            spec_ref[L + r:L + r + rc, :] = (zr * hi + zi * hr).astype(BF16)
        nyq = jnp.sum(alt * z, axis=0, keepdims=True) * hn_ref[...] * (1.0 / (2 * L))
        ys = []
        for r in range(0, L, rc):
            ys.append(_bdot(i_hi_ref[r:r + rc, :], spec_ref[...]) + _bdot(i_lo_ref[r:r + rc, :], spec_ref[...]))
        y = jnp.concatenate(ys, axis=0) + alt * nyq
        z = gates[o] * (y + skip_ref[o:o + 1, :] * z)
    o_ref[0] = z.astype(BF16)


def _hyena_call(u3, L, conv_w, conv_b, spectrum, skip):
    B = u3.shape[0]
    nct = HY_WIDTH // HY_CT
    hr, hi, hn = spectrum
    (f_hi, f_lo), (i_hi, i_lo), _ = _dft_mats(L)
    off = HY_OFF // HY_CT
    u_spec = lambda part: pl.BlockSpec((1, L, HY_CT), lambda b, ct: (b, 0, off + part * nct + ct))
    cw_spec = lambda part: pl.BlockSpec((3, HY_CT), lambda b, ct: (0, part * nct + ct))
    cb_spec = lambda part: pl.BlockSpec((1, HY_CT), lambda b, ct: (0, part * nct + ct))
    h_spec = lambda o: pl.BlockSpec((L, HY_CT), lambda b, ct: (0, o * nct + ct))
    n_spec = lambda o: pl.BlockSpec((1, HY_CT), lambda b, ct: (0, o * nct + ct))
    cb = conv_b.reshape(1, -1)
    return pl.pallas_call(
        _hyena_kernel,
        grid=(B, nct),
        in_specs=[
            u_spec(0), u_spec(1), u_spec(2),
            cw_spec(0), cw_spec(1), cw_spec(2),
            cb_spec(0), cb_spec(1), cb_spec(2),
            h_spec(0), h_spec(0), h_spec(1), h_spec(1), n_spec(0), n_spec(1),
            pl.BlockSpec((HY_ORDER, HY_CT), lambda b, ct: (0, ct)),
            _const_spec((2 * L, L)), _const_spec((2 * L, L)),
            _const_spec((L, 2 * L)), _const_spec((L, 2 * L)),
        ],
        out_specs=pl.BlockSpec((1, L, HY_CT), lambda b, ct: (b, 0, ct)),
        out_shape=jax.ShapeDtypeStruct((B, L, HY_WIDTH), BF16),
        scratch_shapes=[pltpu.VMEM((2 * L, HY_CT), BF16)],
        compiler_params=_cparams(("arbitrary", "arbitrary")),
        name="hyena_mix",
    )(u3, u3, u3, conv_w, conv_w, conv_w, cb, cb, cb, hr, hi, hr, hi, hn, hn, skip,
      f_hi, f_lo, i_hi, i_lo)


def _sink_softmax_pv(scores, values, sink):
    m = sink
    for s in scores:
        m = jnp.maximum(m, jnp.max(s, axis=-1, keepdims=True))
    ps = [jnp.exp(s - m) for s in scores]
    den = jnp.exp(sink - m)
    for p in ps:
        den = den + jnp.sum(p, axis=-1, keepdims=True)
    inv = 1.0 / den
    out = None
    for p, v in zip(ps, values):
        t = _bdot((p * inv).astype(BF16), v)
        out = t if out is None else out + t
    return out


def _ctx_attn_kernel(sink_ref, q_ref, k_ref, v_ref, o_ref):
    h = pl.program_id(1)
    k = k_ref[0].astype(BF16)
    v = v_ref[0].astype(BF16)
    for g in range(GQA):
        cols = slice(g * HEAD_DIM, (g + 1) * HEAD_DIM)
        q = q_ref[0, :, cols].astype(BF16)
        s = lax.dot_general(q, k, (((1,), (1,)), ((), ())), preferred_element_type=F32) * SCALE
        o = _sink_softmax_pv([s], [v], sink_ref[h * GQA + g])
        o_ref[0, :, cols] = o.astype(BF16)


def _ctx_attn_call(u3, sink):
    B, S, _ = u3.shape
    gw = GQA * HEAD_DIM
    return pl.pallas_call(
        _ctx_attn_kernel,
        grid=(B, N_KV_HEADS),
        in_specs=[
            pl.BlockSpec(memory_space=pltpu.SMEM),
            pl.BlockSpec((1, S, gw), lambda b, h: (b, 0, Q_OFF // gw + h)),
            pl.BlockSpec((1, S, HEAD_DIM), lambda b, h: (b, 0, K_OFF // HEAD_DIM + h)),
            pl.BlockSpec((1, S, HEAD_DIM), lambda b, h: (b, 0, V_OFF // HEAD_DIM + h)),
        ],
        out_specs=pl.BlockSpec((1, S, gw), lambda b, h: (b, 0, h)),
        out_shape=jax.ShapeDtypeStruct((B, S, ATT_WIDTH), BF16),
        compiler_params=_cparams(("arbitrary", "arbitrary")),
        name="context_attention",
    )(sink, u3, u3, u3)


def _rope(x, cos, sin_signed, low_half):
    half = ROPE_AXIS_DIM // 2
    partner = jnp.where(low_half, pltpu.roll(x, HEAD_DIM - half, axis=1), pltpu.roll(x, half, axis=1))
    return x * cos + partner * sin_signed


def _lat_attn_kernel(sink_ref, q_ref, k_ref, v_ref, ck_ref, cv_ref, cos_ref, sin_ref, o_ref, kr_ref, vb_ref):
    h = pl.program_id(1)
    L = k_ref.shape[1]
    nb = L // BLOCK
    def low_half(rows):
        lane = lax.broadcasted_iota(jnp.int32, (rows, HEAD_DIM), 1)
        return (lane & (ROPE_AXIS_DIM - 1)) < (ROPE_AXIS_DIM // 2)

    kr_ref[...] = _rope(k_ref[0], cos_ref[...], sin_ref[...], low_half(L)).astype(BF16)
    vb_ref[...] = v_ref[0].astype(BF16)
    ck = ck_ref[0, 0].astype(BF16)
    cv = cv_ref[0, 0].astype(BF16)
    nt = (((1,), (1,)), ((), ()))
    for g in range(GQA):
        cols = slice(g * HEAD_DIM, (g + 1) * HEAD_DIM)
        sink = sink_ref[h * GQA + g]
        for n in range(nb):
            rows = slice(n * BLOCK, (n + 1) * BLOCK)
            q = _rope(q_ref[0, rows, cols], cos_ref[rows, :], sin_ref[rows, :], low_half(BLOCK)).astype(BF16)
            lo = max(n - 1, 0) * BLOCK
            hi = min(n + 2, nb) * BLOCK
            s_lat = lax.dot_general(q, kr_ref[lo:hi, :], nt, preferred_element_type=F32) * SCALE
            qpos = n * BLOCK + lax.broadcasted_iota(jnp.int32, s_lat.shape, 0)
            kpos = lo + lax.broadcasted_iota(jnp.int32, s_lat.shape, 1)
            s_lat = jnp.where(jnp.abs(kpos - qpos) <= WINDOW, s_lat, -1e30)
            s_ctx = lax.dot_general(q, ck, nt, preferred_element_type=F32) * SCALE
            o = _sink_softmax_pv([s_lat, s_ctx], [vb_ref[lo:hi, :], cv], sink)
            o_ref[0, rows, cols] = o.astype(BF16)


def _lat_attn_call(u3, cache_k_l, cache_v_l, layer, sink, cos, sin_signed):
    B, L, _ = u3.shape
    P = cache_k_l.shape[2]
    gw = GQA * HEAD_DIM
    c_spec = pl.BlockSpec((1, 1, P, HEAD_DIM), lambda b, h: (b, layer, 0, h))
    return pl.pallas_call(
        _lat_attn_kernel,
        grid=(B, N_KV_HEADS),
        in_specs=[
            pl.BlockSpec(memory_space=pltpu.SMEM),
            pl.BlockSpec((1, L, gw), lambda b, h: (b, 0, Q_OFF // gw + h)),
            pl.BlockSpec((1, L, HEAD_DIM), lambda b, h: (b, 0, K_OFF // HEAD_DIM + h)),
            pl.BlockSpec((1, L, HEAD_DIM), lambda b, h: (b, 0, V_OFF // HEAD_DIM + h)),
            c_spec, c_spec,
            _const_spec((L, HEAD_DIM)), _const_spec((L, HEAD_DIM)),
        ],
        out_specs=pl.BlockSpec((1, L, gw), lambda b, h: (b, 0, h)),
        out_shape=jax.ShapeDtypeStruct((B, L, ATT_WIDTH), BF16),
        scratch_shapes=[pltpu.VMEM((L, HEAD_DIM), BF16), pltpu.VMEM((L, HEAD_DIM), BF16)],
        compiler_params=_cparams(("arbitrary", "arbitrary")),
        name="latent_attention",
    )(sink, u3, u3, u3, cache_k_l, cache_v_l, cos, sin_signed)


def _rope_tables(L):
    rows = L // GRID_W
    row_pos = jnp.repeat(jnp.arange(rows), GRID_W)
    col_pos = jnp.arange(L) % GRID_W
    half = ROPE_AXIS_DIM // 2
    inv = ROPE_THETA ** (-jnp.arange(half, dtype=F32) * 2.0 / ROPE_AXIS_DIM)
    cos_parts, sin_parts = [], []
    for pos in (row_pos, col_pos):
        ang = pos.astype(F32)[:, None] * inv[None, :]
        cos_parts += [jnp.cos(ang), jnp.cos(ang)]
        sin_parts += [-jnp.sin(ang), jnp.sin(ang)]
    return jnp.concatenate(cos_parts, axis=-1), jnp.concatenate(sin_parts, axis=-1)


def _outproj_kernel(f_ref, hy_ref, att_ref, wf_ref, wh_ref, wa_ref, o_ref):
    o_ref[...] = (_bdot(f_ref[...], wf_ref[...]) + _bdot(hy_ref[...], wh_ref[...])
                  + _bdot(att_ref[...], wa_ref[...]))


def _outproj_call(f, hy, att, w_out, layer):
    T = f.shape[0]
    assert F_WIDTH == HY_WIDTH and ATT_WIDTH == F_WIDTH + HY_WIDTH
    return pl.pallas_call(
        _outproj_kernel,
        grid=(T // TM_MM, D_MODEL // TN_MM),
        in_specs=[
            pl.BlockSpec((TM_MM, F_WIDTH), lambda i, j: (i, 0)),
            pl.BlockSpec((TM_MM, HY_WIDTH), lambda i, j: (i, 0)),
            pl.BlockSpec((TM_MM, ATT_WIDTH), lambda i, j: (i, 0)),
            pl.BlockSpec((None, F_WIDTH, TN_MM), lambda i, j: (layer, 0, j)),
            pl.BlockSpec((None, HY_WIDTH, TN_MM), lambda i, j: (layer, 1, j)),
            pl.BlockSpec((None, ATT_WIDTH, TN_MM), lambda i, j: (layer, 1, j)),
        ],
        out_specs=pl.BlockSpec((TM_MM, TN_MM), lambda i, j: (i, j)),
        out_shape=jax.ShapeDtypeStruct((T, D_MODEL), F32),
        compiler_params=_cparams(("arbitrary", "arbitrary")),
        name="outproj",
    )(f, hy, att, w_out, w_out, w_out)


def _ffn_kernel(seq_len, need_halo, *refs):
    if need_halo:
        (h_ref, hp_ref, hn_ref, wg_ref, wv_ref, cwg_ref, cwv_ref, cbg_ref, cbv_ref, wd_ref, o_ref,
         u_ref, act_ref, hs_ref) = refs
    else:
        (h_ref, wg_ref, wv_ref, cwg_ref, cwv_ref, cbg_ref, cbv_ref, wd_ref, o_ref, u_ref, act_ref) = refs
    i = pl.program_id(0)
    j = pl.program_id(1)
    tm = h_ref.shape[0]
    seg = min(seq_len, tm)
    n_seg = tm // seg
    n_rows = u_ref.shape[2]
    base = [SUBLANE + s * (seg + SUBLANE) for s in range(n_seg)]

    @pl.when(j == 0)
    def _():
        o_ref[...] = jnp.zeros(o_ref.shape, F32)
        u_ref[...] = jnp.zeros(u_ref.shape, F32)
        act_ref[...] = jnp.zeros(act_ref.shape, BF16)
        if need_halo:
            def body(rows):
                hs_ref[rows, :] = h_ref[rows, :]
            _for_row_chunks(tm, body)
            hs_ref[tm:tm + BF16_ROWS, :] = hp_ref[...]
            hs_ref[tm + BF16_ROWS:, :] = hn_ref[...]

    lhs_ref = hs_ref if need_halo else h_ref
    if need_halo:
        keep_prev = ((i * tm) & (seq_len - 1) != 0).astype(F32)
        keep_next = (((i + 1) * tm) & (seq_len - 1) != 0).astype(F32)
    live = (j > 0).astype(F32)

    def conv_piece(slot, p):
        rp = tm // FFN_PIECES
        s, r0 = divmod(p * rp, seg)
        lo = base[s] + r0

        def conv(part, cw_ref, cb_ref):
            win = lambda off: u_ref[1 - slot, part, lo + off:lo + off + rp, :]
            return win(-1) * cw_ref[0:1, :] + win(0) * cw_ref[1:2, :] + win(1) * cw_ref[2:3, :] + cb_ref[...]

        gate = conv(0, cwg_ref, cbg_ref)
        val = conv(1, cwv_ref, cbv_ref)
        act_ref[1 - slot, p * rp:(p + 1) * rp, :] = (_silu(gate) * (val * live)).astype(BF16)

    def step(slot):
        cp = D_MODEL // FFN_PIECES
        for p in range(FFN_PIECES):
            cols = slice(p * cp, (p + 1) * cp)
            o_ref[:, cols] += _bdot(act_ref[slot], wd_ref[:, cols])
            conv_piece(slot, p)

        for part, w_ref in enumerate((wg_ref, wv_ref)):
            ua = _bdot(lhs_ref[...], w_ref[...])
            for s in range(n_seg):
                u_ref[slot, part, base[s]:base[s] + seg, :] = ua[s * seg:(s + 1) * seg]
            if need_halo:
                u_ref[slot, part, 0:SUBLANE, :] = ua[tm + SUBLANE:tm + 2 * SUBLANE] * keep_prev
                u_ref[slot, part, n_rows - SUBLANE:n_rows, :] = ua[tm + 2 * SUBLANE:tm + 3 * SUBLANE] * keep_next

    for slot in (0, 1):
        pl.when((j & 1) == slot)(functools.partial(step, slot))


def _ffn_call(h, seq_len, w_up, conv_w, conv_b, w_down, layer):
    T = h.shape[0]
    nf = D_FF // TF_FFN
    need_halo = TM_FFN % seq_len != 0
    halo_blocks = TM_FFN // BF16_ROWS
    n_halo = T // BF16_ROWS
    cb = conv_b.reshape(DEPTH, 1, -1)
    wspec = lambda shape, imap: pl.BlockSpec((None,) + shape, imap)
    in_specs = [pl.BlockSpec((TM_FFN, D_MODEL), lambda i, j: (i, 0))]
    args = [h]
    seg = min(seq_len, TM_FFN)
    u_rows = SUBLANE + (TM_FFN // seg) * (seg + SUBLANE)
    scratch = [pltpu.VMEM((2, 2, u_rows, TF_FFN), F32), pltpu.VMEM((2, TM_FFN, TF_FFN), BF16)]
    if need_halo:
        in_specs += [
            pl.BlockSpec((BF16_ROWS, D_MODEL), lambda i, j: (jnp.maximum(i * halo_blocks - 1, 0), 0)),
            pl.BlockSpec((BF16_ROWS, D_MODEL), lambda i, j: (jnp.minimum((i + 1) * halo_blocks, n_halo - 1), 0)),
        ]
        args += [h, h]
        scratch.append(pltpu.VMEM((TM_FFN + 2 * BF16_ROWS, D_MODEL), BF16))
    up_chunk = lambda j: jnp.minimum(j, nf - 1)
    cv_chunk = lambda j: jnp.clip(j - 1, 0, nf - 1)
    dn_chunk = lambda j: jnp.clip(j - 2, 0, nf - 1)
    in_specs += [
        wspec((D_MODEL, TF_FFN), lambda i, j: (layer, 0, up_chunk(j))),
        wspec((D_MODEL, TF_FFN), lambda i, j: (layer, 0, nf + up_chunk(j))),
        wspec((3, TF_FFN), lambda i, j: (layer, 0, cv_chunk(j))),
        wspec((3, TF_FFN), lambda i, j: (layer, 0, nf + cv_chunk(j))),
        wspec((1, TF_FFN), lambda i, j: (layer, 0, cv_chunk(j))),
        wspec((1, TF_FFN), lambda i, j: (layer, 0, nf + cv_chunk(j))),
        wspec((TF_FFN, D_MODEL), lambda i, j: (layer, dn_chunk(j), 0)),
    ]
    args += [w_up, w_up, conv_w, conv_w, cb, cb, w_down]
    return pl.pallas_call(
        functools.partial(_ffn_kernel, seq_len, need_halo),
        grid=(T // TM_FFN, nf + 2),
        in_specs=in_specs,
        out_specs=pl.BlockSpec((TM_FFN, D_MODEL), lambda i, j: (i, 0)),
        out_shape=jax.ShapeDtypeStruct((T, D_MODEL), F32),
        scratch_shapes=scratch,
        compiler_params=_cparams(("arbitrary", "arbitrary")),
        name="conv_ffn",
    )(*args)


def _layer(x, h, B, L, layer, mods, modrow_of_tile, big, p, attend):
    T = B * L
    u3 = _inproj_call(h, big["w_in"], layer).reshape(B, L, IN_WIDTH)
    f = _fourier_call(u3, L)
    spectrum = _filters_call(L, p["flt_w1"], p["flt_b1"], p["flt_w2"], p["flt_b2"], p["flt_w3"], p["flt_freq"])
    hy = _hyena_call(u3, L, p["hy_conv_w"], p["hy_conv_b"], spectrum, p["hy_skip"])
    att = attend(u3)
    proj = _outproj_call(f.reshape(T, F_WIDTH), hy.reshape(T, HY_WIDTH), att.reshape(T, ATT_WIDTH),
                         big["w_out"], layer)
    x, h2 = _resid_ln_call(x, proj, mods, layer, 2, big["ln1_g"], big["ln1_b"], modrow_of_tile,
                           h_layer=layer, h_row=3)
    ff = _ffn_call(h2, L, big["ffn_w_up"], big["ffn_conv_w"], big["ffn_conv_b"], big["ffn_w_down"], layer)
    if layer + 1 < DEPTH:
        x, h_next = _resid_ln_call(x, ff, mods, layer, 5, big["ln2_g"], big["ln2_b"], modrow_of_tile,
                                   h_layer=layer + 1, h_row=0)
    else:
        x, h_next = _resid_ln_call(x, ff, mods, layer, 5, big["ln2_g"], big["ln2_b"], modrow_of_tile)
    return x, h_next, u3


def kernel(x_prompt, x_sample, cache_k, cache_v, c, c_ctx, w_mod, b_mod, w_in, w_out, attn_sink,
           hy_conv_w, hy_conv_b, flt_w1, flt_b1, flt_w2, flt_b2, flt_w3, flt_freq, hy_skip,
           ln1_g, ln1_b, ffn_w_up, ffn_conv_w, ffn_conv_b, ffn_w_down, ln2_g, ln2_b):
    Bp, S, _ = x_prompt.shape
    Bs, Ls, _ = x_sample.shape
    past = cache_k.shape[2]

    cvec = jnp.zeros((MOD_ROWS, D_MODEL), F32).at[0].set(c_ctx).at[1:1 + Bs].set(c)
    mods = _mod_call(cvec, w_mod, b_mod).reshape(DEPTH, MOD_ROWS, 6, D_MODEL)

    big = dict(w_in=w_in.astype(BF16), w_out=w_out.astype(BF16), ffn_w_up=ffn_w_up.astype(BF16),
               ffn_w_down=ffn_w_down.astype(BF16), ffn_conv_w=ffn_conv_w, ffn_conv_b=ffn_conv_b,
               ln1_g=ln1_g, ln1_b=ln1_b, ln2_g=ln2_g, ln2_b=ln2_b)
    ck = cache_k.reshape(Bs, DEPTH, past, KV_WIDTH)
    cv = cache_v.reshape(Bs, DEPTH, past, KV_WIDTH)
    cos, sin_signed = _rope_tables(Ls)

    ctx_row = lambda i, tm: 0
    lat_row = lambda i, tm: 1 + i * tm // Ls

    xp = x_prompt.reshape(Bp * S, D_MODEL)
    xs = x_sample.reshape(Bs * Ls, D_MODEL)
    hp = _ln_mod_call(xp, mods, 0, ctx_row)
    hs = _ln_mod_call(xs, mods, 0, lat_row)
    ctx_k, ctx_v = [], []
    for l in range(DEPTH):
        p = dict(flt_w1=flt_w1[l], flt_b1=flt_b1[l], flt_w2=flt_w2[l], flt_b2=flt_b2[l], flt_w3=flt_w3[l],
                 flt_freq=flt_freq[l], hy_conv_w=hy_conv_w[l], hy_conv_b=hy_conv_b[l], hy_skip=hy_skip[l])
        sink = attn_sink[l]
        xp, hp, up3 = _layer(xp, hp, Bp, S, l, mods, ctx_row, big, p, lambda u3: _ctx_attn_call(u3, sink))
        ctx_k.append(up3[:, :, K_OFF:K_OFF + KV_WIDTH].reshape(Bp, S, N_KV_HEADS, HEAD_DIM))
        ctx_v.append(up3[:, :, V_OFF:V_OFF + KV_WIDTH].reshape(Bp, S, N_KV_HEADS, HEAD_DIM))
        xs, hs, _ = _layer(xs, hs, Bs, Ls, l, mods, lat_row, big, p,
                           lambda u3: _lat_attn_call(u3, ck, cv, l, sink, cos, sin_signed))

    return (xp.reshape(Bp, S, D_MODEL), xs.reshape(Bs, Ls, D_MODEL),
            jnp.stack(ctx_k, axis=1), jnp.stack(ctx_v, axis=1))
```

```python
import functools
import math

import ml_dtypes
import numpy as np
import jax
import jax.numpy as jnp
from jax import lax
from jax.experimental import pallas as pl
from jax.experimental.pallas import tpu as pltpu

F32 = jnp.float32
BF16 = jnp.bfloat16

D_MODEL = 4096
DEPTH = 2
GRID_W = 64
F_WIDTH = D_MODEL // 4
F_GROUPS = 8
F_GROUP_W = F_WIDTH // F_GROUPS
HY_WIDTH = D_MODEL // 4
HY_ORDER = 2
N_BANDS = 16
POS_EMB = 1 + 2 * N_BANDS
FILTER_HID = 64
DECAY_TARGET = 1e-2
MIN_DECAY = math.log(DECAY_TARGET) / 1.5
MAX_DECAY = math.log(DECAY_TARGET) / 0.3
HEAD_DIM = 128
N_Q_HEADS = (D_MODEL // 2) // HEAD_DIM
N_KV_HEADS = N_Q_HEADS // 4
GQA = N_Q_HEADS // N_KV_HEADS
ATT_WIDTH = N_Q_HEADS * HEAD_DIM
KV_WIDTH = N_KV_HEADS * HEAD_DIM
WINDOW = 128
BLOCK = 128
SCALE = HEAD_DIM ** -0.5
ROPE_THETA = 10000.0
ROPE_AXIS_DIM = HEAD_DIM // 2
D_FF = 256 * ((8 * D_MODEL // 3 + 255) // 256)
LN_EPS = 1e-5
ALPHA = (2 * DEPTH) ** 0.25
IN_WIDTH = F_WIDTH + 3 * HY_WIDTH + ATT_WIDTH + 2 * KV_WIDTH
MIX_WIDTH = F_WIDTH + HY_WIDTH + ATT_WIDTH

HY_OFF = F_WIDTH
Q_OFF = F_WIDTH + 3 * HY_WIDTH
K_OFF = Q_OFF + ATT_WIDTH
V_OFF = K_OFF + KV_WIDTH

MOD_ROWS = 16
LANE = 128
SUBLANE = 8
VMEM_LIMIT = 56 * 1024 * 1024

BF16_ROWS = 16
TM_MM = 1024
TN_MM = 512
TM_LN = 256
TM_FFN = 1024
TF_FFN = 256
FFN_UP_ROWS = 256
FFN_PIECES = 16
HY_CT = 256
TN_MOD = 512
TK_MOD = 512
ROW_CHUNK = 32


def _cparams(sem):
    return pltpu.CompilerParams(dimension_semantics=sem, vmem_limit_bytes=VMEM_LIMIT)


def _const_spec(shape):
    return pl.BlockSpec(shape, lambda *_: (0,) * len(shape), pipeline_mode=pl.Buffered(1))


def _bdot(a, b):
    return jnp.dot(a, b, preferred_element_type=F32)


def _ln_rows(x):
    mu = jnp.mean(x, axis=-1, keepdims=True)
    xc = x - mu
    var = jnp.mean(xc * xc, axis=-1, keepdims=True)
    return xc * lax.rsqrt(var + LN_EPS)


def _silu(x):
    return x / (1.0 + jnp.exp(-x))


def _for_row_chunks(n_rows, body):
    def step(c, carry):
        body(pl.ds(pl.multiple_of(c * ROW_CHUNK, ROW_CHUNK), ROW_CHUNK))
        return carry
    lax.fori_loop(0, n_rows // ROW_CHUNK, step, 0)


def _hi_lo(x):
    hi = x.astype(ml_dtypes.bfloat16)
    lo = (x - hi.astype(np.float64)).astype(ml_dtypes.bfloat16)
    return jnp.asarray(hi), jnp.asarray(lo)


@functools.lru_cache(maxsize=None)
def _dft_mats(L):
    n = np.arange(L)
    ang = np.pi * ((n[:, None] * n[None, :]) % (2 * L)) / L
    fwd = np.concatenate([np.cos(ang), -np.sin(ang)], axis=0)
    wgt = np.full(L, 2.0)
    wgt[0] = 1.0
    inv = (fwd * np.concatenate([wgt, wgt])[:, None]).T / (2 * L)
    ang2 = 2.0 * np.pi * ((n[:, None] * n[None, :]) % L) / L
    fm = np.concatenate([np.cos(ang2), -np.sin(ang2)], axis=1) / np.sqrt(L)
    return _hi_lo(fwd), _hi_lo(inv), _hi_lo(fm)


@functools.lru_cache(maxsize=None)
def _chan_dft():
    n = np.arange(F_GROUP_W)
    ang = 2.0 * np.pi * ((n[:, None] * n[None, :]) % F_GROUP_W) / F_GROUP_W
    return _hi_lo(np.concatenate([np.cos(ang), np.sin(ang)], axis=1) / np.sqrt(F_GROUP_W))


def _mod_kernel(c_ref, w_ref, b_ref, o_ref):
    acc = b_ref[0] + jnp.zeros((MOD_ROWS, TN_MOD), F32)
    for k in range(0, D_MODEL, TK_MOD):
        s = _silu(c_ref[:, k:k + TK_MOD]).astype(BF16)
        acc = acc + _bdot(s, w_ref[0, k:k + TK_MOD, :].astype(BF16))
    o_ref[0] = acc


def _mod_call(cvec, w_mod, b_mod):
    n_out = w_mod.shape[-1]
    return pl.pallas_call(
        _mod_kernel,
        grid=(DEPTH, n_out // TN_MOD),
        in_specs=[
            pl.BlockSpec((MOD_ROWS, D_MODEL), lambda l, j: (0, 0)),
            pl.BlockSpec((1, D_MODEL, TN_MOD), lambda l, j: (l, 0, j)),
            pl.BlockSpec((1, 1, TN_MOD), lambda l, j: (l, 0, j)),
        ],
        out_specs=pl.BlockSpec((1, MOD_ROWS, TN_MOD), lambda l, j: (l, 0, j)),
        out_shape=jax.ShapeDtypeStruct((DEPTH, MOD_ROWS, n_out), F32),
        compiler_params=_cparams(("arbitrary", "arbitrary")),
        name="adaln_mod",
    )(cvec, w_mod, b_mod.reshape(DEPTH, 1, n_out))


def _inproj_kernel(h_ref, w_ref, o_ref):
    o_ref[...] = _bdot(h_ref[...], w_ref[...])


def _inproj_call(h, w_in, layer):
    T = h.shape[0]
    return pl.pallas_call(
        _inproj_kernel,
        grid=(T // TM_MM, IN_WIDTH // TN_MM),
        in_specs=[
            pl.BlockSpec((TM_MM, D_MODEL), lambda i, j: (i, 0)),
            pl.BlockSpec((None, D_MODEL, TN_MM), lambda i, j: (layer, 0, j)),
        ],
        out_specs=pl.BlockSpec((TM_MM, TN_MM), lambda i, j: (i, j)),
        out_shape=jax.ShapeDtypeStruct((T, IN_WIDTH), F32),
        compiler_params=_cparams(("arbitrary", "arbitrary")),
        name="inproj",
    )(h, w_in)


def _mod_spec(layer, modrow_of_tile, tm):
    return pl.BlockSpec((None, None, 6, D_MODEL), lambda i: (layer, modrow_of_tile(i, tm), 0, 0))


def _ln_mod_kernel(x_ref, mod_ref, h_ref):
    def body(rows):
        h = _ln_rows(x_ref[rows, :]) * (1.0 + mod_ref[1:2, :]) + mod_ref[0:1, :]
        h_ref[rows, :] = h.astype(BF16)
    _for_row_chunks(TM_LN, body)


def _ln_mod_call(x, mods, layer, modrow_of_tile):
    T = x.shape[0]
    return pl.pallas_call(
        _ln_mod_kernel,
        grid=(T // TM_LN,),
        in_specs=[pl.BlockSpec((TM_LN, D_MODEL), lambda i: (i, 0)), _mod_spec(layer, modrow_of_tile, TM_LN)],
        out_specs=pl.BlockSpec((TM_LN, D_MODEL), lambda i: (i, 0)),
        out_shape=jax.ShapeDtypeStruct((T, D_MODEL), BF16),
        compiler_params=_cparams(("arbitrary",)),
        name="ln_modulate",
    )(x, mods)


def _resid_ln_kernel(gate_row, h_row, x_ref, d_ref, modg_ref, g_ref, b_ref, *rest):
    if h_row is None:
        (xo_ref,) = rest
    else:
        modh_ref, xo_ref, ho_ref = rest

    def body(rows):
        y = ALPHA * x_ref[rows, :] + modg_ref[gate_row:gate_row + 1, :] * d_ref[rows, :]
        xn = _ln_rows(y) * g_ref[...] + b_ref[...]
        xo_ref[rows, :] = xn
        if h_row is not None:
            h = _ln_rows(xn) * (1.0 + modh_ref[h_row + 1:h_row + 2, :]) + modh_ref[h_row:h_row + 1, :]
            ho_ref[rows, :] = h.astype(BF16)
    _for_row_chunks(TM_LN, body)


def _resid_ln_call(x, d, mods, layer, gate_row, ln_g, ln_b, modrow_of_tile, h_layer=None, h_row=None):
    T = x.shape[0]
    tile = pl.BlockSpec((TM_LN, D_MODEL), lambda i: (i, 0))
    vec = pl.BlockSpec((None, 1, D_MODEL), lambda i: (layer, 0, 0))
    in_specs = [tile, tile, _mod_spec(layer, modrow_of_tile, TM_LN), vec, vec]
    args = [x, d, mods, ln_g.reshape(DEPTH, 1, D_MODEL), ln_b.reshape(DEPTH, 1, D_MODEL)]
    out_specs = [tile]
    out_shape = [jax.ShapeDtypeStruct((T, D_MODEL), F32)]
    if h_row is not None:
        in_specs.append(_mod_spec(h_layer, modrow_of_tile, TM_LN))
        args.append(mods)
        out_specs.append(tile)
        out_shape.append(jax.ShapeDtypeStruct((T, D_MODEL), BF16))
    out = pl.pallas_call(
        functools.partial(_resid_ln_kernel, gate_row, h_row),
        grid=(T // TM_LN,),
        in_specs=in_specs,
        out_specs=out_specs,
        out_shape=out_shape,
        compiler_params=_cparams(("arbitrary",)),
        name="residual_ln",
    )(*args)
    return out if h_row is not None else (out[0], None)


def _fourier_kernel(u_ref, cc_hi_ref, cc_lo_ref, fm_hi_ref, fm_lo_ref, o_ref, p_ref):
    L = u_ref.shape[1]
    rc = min(L, 256)
    for g in range(F_GROUPS):
        cols = slice(g * F_GROUP_W, (g + 1) * F_GROUP_W)
        for r in range(0, L, rc):
            ug = u_ref[0, r:r + rc, cols].astype(BF16)
            a = _bdot(ug, cc_hi_ref[...]) + _bdot(ug, cc_lo_ref[...])
            p_ref[r:r + rc, cols] = a[:, :F_GROUP_W].astype(BF16)
            p_ref[L + r:L + r + rc, cols] = a[:, F_GROUP_W:].astype(BF16)
    for r in range(0, L, rc):
        y = _bdot(fm_hi_ref[r:r + rc, :], p_ref[...]) + _bdot(fm_lo_ref[r:r + rc, :], p_ref[...])
        o_ref[0, r:r + rc, :] = y.astype(BF16)


def _fourier_call(u3, L):
    B = u3.shape[0]
    _, _, (fm_hi, fm_lo) = _dft_mats(L)
    cc_hi, cc_lo = _chan_dft()
    return pl.pallas_call(
        _fourier_kernel,
        grid=(B,),
        in_specs=[
            pl.BlockSpec((1, L, F_WIDTH), lambda b: (b, 0, 0)),
            _const_spec((F_GROUP_W, 2 * F_GROUP_W)),
            _const_spec((F_GROUP_W, 2 * F_GROUP_W)),
            _const_spec((L, 2 * L)),
            _const_spec((L, 2 * L)),
        ],
        out_specs=pl.BlockSpec((1, L, F_WIDTH), lambda b: (b, 0, 0)),
        out_shape=jax.ShapeDtypeStruct((B, L, F_WIDTH), BF16),
        scratch_shapes=[pltpu.VMEM((2 * L, F_WIDTH), BF16)],
        compiler_params=_cparams(("arbitrary",)),
        name="fourier_mix",
    )(u3, cc_hi, cc_lo, fm_hi, fm_lo)


def _filter_taps_kernel(feat_ref, w1_ref, b1_ref, w2_ref, b2_ref, freq_ref, w3f_ref, w3b_ref, decay_ref,
                        fwd_ref, bsh_ref):
    L = feat_ref.shape[0]
    hp = lax.Precision.HIGHEST
    feats = feat_ref[...]
    freq = freq_ref[...]
    h = jnp.sin(freq * (jnp.dot(feats, w1_ref[...], precision=hp, preferred_element_type=F32) + b1_ref[...]))
    h = jnp.sin(freq * (jnp.dot(h, w2_ref[...], precision=hp, preferred_element_type=F32) + b2_ref[...]))
    win = jnp.exp(-feats[:, 0:1] * decay_ref[...])
    fwd = jnp.dot(h, w3f_ref[...], precision=hp, preferred_element_type=F32) * win
    bwd = jnp.dot(h, w3b_ref[...], precision=hp, preferred_element_type=F32) * win
    row = lax.broadcasted_iota(jnp.int32, bwd.shape, 0)
    bsh = jnp.where(row == 0, 0.0, pltpu.roll(bwd, 1, axis=0))
    norm = jnp.sum(jnp.abs(fwd), axis=0, keepdims=True) + jnp.sum(jnp.abs(bsh), axis=0, keepdims=True)
    fwd_ref[...] = fwd / norm
    bsh_ref[...] = bsh / norm


def _filter_spectrum_kernel(fwd_ref, bsh_ref, f_hi_ref, f_lo_ref, hr_ref, hi_ref, hn_ref):
    L = fwd_ref.shape[0]
    fwd = fwd_ref[...]
    bsh = bsh_ref[...]
    both = jnp.concatenate([fwd, bsh], axis=1)
    b_hi = both.astype(BF16)
    b_lo = (both - b_hi.astype(F32)).astype(BF16)
    ct = fwd.shape[1]
    rc = min(L, 256)
    for r in range(0, L, rc):
        def spec(rows):
            return (_bdot(f_hi_ref[rows, :], b_hi) + _bdot(f_lo_ref[rows, :], b_hi)
                    + _bdot(f_hi_ref[rows, :], b_lo))
        c = spec(slice(r, r + rc))
        s = spec(slice(L + r, L + r + rc))
        hr_ref[r:r + rc, :] = c[:, :ct] + c[:, ct:]
        hi_ref[r:r + rc, :] = s[:, :ct] - s[:, ct:]
    row = lax.broadcasted_iota(jnp.int32, fwd.shape, 0)
    alt = jnp.where((row & 1) == 0, 1.0, -1.0)
    hn_ref[...] = jnp.sum(alt * (fwd + bsh), axis=0, keepdims=True)


def _filters_call(L, w1, b1, w2, b2, w3, freq):
    t = np.arange(L, dtype=np.float32)
    t_norm = t / np.float32(max(L - 1, 1))
    w = np.float32(2.0 * math.pi) * t / np.float32(L)
    bands = np.linspace(1e-4, N_BANDS - 1, N_BANDS, dtype=np.float32)
    feats = np.zeros((L, LANE), np.float32)
    feats[:, 0] = t_norm
    feats[:, 1:1 + N_BANDS] = np.cos(w[:, None] * bands[None])
    feats[:, 1 + N_BANDS:POS_EMB] = np.sin(w[:, None] * bands[None])
    decay = np.abs(np.linspace(MIN_DECAY, MAX_DECAY, HY_WIDTH, dtype=np.float32))[None, :]
    w1p = jnp.zeros((LANE, FILTER_HID), F32).at[:POS_EMB].set(w1)
    nct = HY_WIDTH // HY_CT
    ocw = HY_ORDER * HY_WIDTH
    row = lambda v: v.reshape(1, -1)
    small = lambda shape: pl.BlockSpec(shape, lambda o, ct: (0, 0))
    fwd, bsh = pl.pallas_call(
        _filter_taps_kernel,
        grid=(HY_ORDER, nct),
        in_specs=[
            small((L, LANE)), small((LANE, FILTER_HID)), small((1, FILTER_HID)),
            small((FILTER_HID, FILTER_HID)), small((1, FILTER_HID)), small((1, FILTER_HID)),
            pl.BlockSpec((FILTER_HID, HY_CT), lambda o, ct: (0, o * 2 * nct + ct)),
            pl.BlockSpec((FILTER_HID, HY_CT), lambda o, ct: (0, o * 2 * nct + nct + ct)),
            pl.BlockSpec((1, HY_CT), lambda o, ct: (0, ct)),
        ],
        out_specs=[pl.BlockSpec((L, HY_CT), lambda o, ct: (0, o * nct + ct))] * 2,
        out_shape=[jax.ShapeDtypeStruct((L, ocw), F32)] * 2,
        compiler_params=_cparams(("arbitrary", "arbitrary")),
        name="hyena_filter_taps",
    )(jnp.asarray(feats), w1p, row(b1), w2, row(b2), row(freq), w3, w3, jnp.asarray(decay))
    (f_hi, f_lo), _, _ = _dft_mats(L)
    return pl.pallas_call(
        _filter_spectrum_kernel,
        grid=(ocw // HY_CT,),
        in_specs=[
            pl.BlockSpec((L, HY_CT), lambda j: (0, j)),
            pl.BlockSpec((L, HY_CT), lambda j: (0, j)),
            _const_spec((2 * L, L)),
            _const_spec((2 * L, L)),
        ],
        out_specs=[pl.BlockSpec((L, HY_CT), lambda j: (0, j)),
                   pl.BlockSpec((L, HY_CT), lambda j: (0, j)),
                   pl.BlockSpec((1, HY_CT), lambda j: (0, j))],
        out_shape=[jax.ShapeDtypeStruct((L, ocw), F32), jax.ShapeDtypeStruct((L, ocw), F32),
                   jax.ShapeDtypeStruct((1, ocw), F32)],
        compiler_params=_cparams(("arbitrary",)),
        name="hyena_filter_spectrum",
    )(fwd, bsh, f_hi, f_lo)


def _conv3_rows(x, w_ref, b_ref, first, last):
    n = x.shape[0]
    prev = jnp.where(first, 0.0, pltpu.roll(x, 1, axis=0))
    nxt = jnp.where(last, 0.0, pltpu.roll(x, n - 1, axis=0))
    return prev * w_ref[0:1, :] + x * w_ref[1:2, :] + nxt * w_ref[2:3, :] + b_ref[...]


def _hyena_kernel(v_ref, x1_ref, x2_ref, wv_ref, w1_ref, w2_ref, bv_ref, b1_ref, b2_ref,
                  hr0_ref, hi0_ref, hr1_ref, hi1_ref, hn0_ref, hn1_ref, skip_ref,
                  f_hi_ref, f_lo_ref, i_hi_ref, i_lo_ref, o_ref, spec_ref):
    L = v_ref.shape[1]
    rc = min(L, 256)
    row = lax.broadcasted_iota(jnp.int32, (L, HY_CT), 0)
    first = row == 0
    last = row == L - 1
    alt = jnp.where((row & 1) == 0, 1.0, -1.0)
    z = _conv3_rows(v_ref[0], wv_ref, bv_ref, first, last)
    gates = (_conv3_rows(x1_ref[0], w1_ref, b1_ref, first, last),
             _conv3_rows(x2_ref[0], w2_ref, b2_ref, first, last))
    filt = ((hr0_ref, hi0_ref, hn0_ref), (hr1_ref, hi1_ref, hn1_ref))
    for o in range(HY_ORDER):
        hr_ref, hi_ref, hn_ref = filt[o]
        zb = z.astype(BF16)
        for r in range(0, L, rc):
            zr = _bdot(f_hi_ref[r:r + rc, :], zb) + _bdot(f_lo_ref[r:r + rc, :], zb)
            zi = _bdot(f_hi_ref[L + r:L + r + rc, :], zb) + _bdot(f_lo_ref[L + r:L + r + rc, :], zb)
            hr = hr_ref[r:r + rc, :]
            hi = hi_ref[r:r + rc, :]
            spec_ref[r:r + rc, :] = (zr * hr - zi * hi).astype(BF16)
            spec_ref[L + r:L + r + rc, :] = (zr * hi + zi * hr).astype(BF16)
        nyq = jnp.sum(alt * z, axis=0, keepdims=True) * hn_ref[...] * (1.0 / (2 * L))
        ys = []
        for r in range(0, L, rc):
            ys.append(_bdot(i_hi_ref[r:r + rc, :], spec_ref[...]) + _bdot(i_lo_ref[r:r + rc, :], spec_ref[...]))
        y = jnp.concatenate(ys, axis=0) + alt * nyq
        z = gates[o] * (y + skip_ref[o:o + 1, :] * z)
    o_ref[0] = z.astype(BF16)


def _hyena_call(u3, L, conv_w, conv_b, spectrum, skip):
    B = u3.shape[0]
    nct = HY_WIDTH // HY_CT
    hr, hi, hn = spectrum
    (f_hi, f_lo), (i_hi, i_lo), _ = _dft_mats(L)
    off = HY_OFF // HY_CT
    u_spec = lambda part: pl.BlockSpec((1, L, HY_CT), lambda b, ct: (b, 0, off + part * nct + ct))
    cw_spec = lambda part: pl.BlockSpec((3, HY_CT), lambda b, ct: (0, part * nct + ct))
    cb_spec = lambda part: pl.BlockSpec((1, HY_CT), lambda b, ct: (0, part * nct + ct))
    h_spec = lambda o: pl.BlockSpec((L, HY_CT), lambda b, ct: (0, o * nct + ct))
    n_spec = lambda o: pl.BlockSpec((1, HY_CT), lambda b, ct: (0, o * nct + ct))
    cb = conv_b.reshape(1, -1)
    return pl.pallas_call(
        _hyena_kernel,
        grid=(B, nct),
        in_specs=[
            u_spec(0), u_spec(1), u_spec(2),
            cw_spec(0), cw_spec(1), cw_spec(2),
            cb_spec(0), cb_spec(1), cb_spec(2),
            h_spec(0), h_spec(0), h_spec(1), h_spec(1), n_spec(0), n_spec(1),
            pl.BlockSpec((HY_ORDER, HY_CT), lambda b, ct: (0, ct)),
            _const_spec((2 * L, L)), _const_spec((2 * L, L)),
            _const_spec((L, 2 * L)), _const_spec((L, 2 * L)),
        ],
        out_specs=pl.BlockSpec((1, L, HY_CT), lambda b, ct: (b, 0, ct)),
        out_shape=jax.ShapeDtypeStruct((B, L, HY_WIDTH), BF16),
        scratch_shapes=[pltpu.VMEM((2 * L, HY_CT), BF16)],
        compiler_params=_cparams(("arbitrary", "arbitrary")),
        name="hyena_mix",
    )(u3, u3, u3, conv_w, conv_w, conv_w, cb, cb, cb, hr, hi, hr, hi, hn, hn, skip,
      f_hi, f_lo, i_hi, i_lo)


def _sink_softmax_pv(scores, values, sink):
    m = sink
    for s in scores:
        m = jnp.maximum(m, jnp.max(s, axis=-1, keepdims=True))
    ps = [jnp.exp(s - m) for s in scores]
    den = jnp.exp(sink - m)
    for p in ps:
        den = den + jnp.sum(p, axis=-1, keepdims=True)
    inv = 1.0 / den
    out = None
    for p, v in zip(ps, values):
        t = _bdot((p * inv).astype(BF16), v)
        out = t if out is None else out + t
    return out


def _ctx_attn_kernel(sink_ref, q_ref, k_ref, v_ref, o_ref):
    h = pl.program_id(1)
    k = k_ref[0].astype(BF16)
    v = v_ref[0].astype(BF16)
    for g in range(GQA):
        cols = slice(g * HEAD_DIM, (g + 1) * HEAD_DIM)
        q = q_ref[0, :, cols].astype(BF16)
        s = lax.dot_general(q, k, (((1,), (1,)), ((), ())), preferred_element_type=F32) * SCALE
        o = _sink_softmax_pv([s], [v], sink_ref[h * GQA + g])
        o_ref[0, :, cols] = o.astype(BF16)


def _ctx_attn_call(u3, sink):
    B, S, _ = u3.shape
    gw = GQA * HEAD_DIM
    return pl.pallas_call(
        _ctx_attn_kernel,
        grid=(B, N_KV_HEADS),
        in_specs=[
            pl.BlockSpec(memory_space=pltpu.SMEM),
            pl.BlockSpec((1, S, gw), lambda b, h: (b, 0, Q_OFF // gw + h)),
            pl.BlockSpec((1, S, HEAD_DIM), lambda b, h: (b, 0, K_OFF // HEAD_DIM + h)),
            pl.BlockSpec((1, S, HEAD_DIM), lambda b, h: (b, 0, V_OFF // HEAD_DIM + h)),
        ],
        out_specs=pl.BlockSpec((1, S, gw), lambda b, h: (b, 0, h)),
        out_shape=jax.ShapeDtypeStruct((B, S, ATT_WIDTH), BF16),
        compiler_params=_cparams(("arbitrary", "arbitrary")),
        name="context_attention",
    )(sink, u3, u3, u3)


def _rope(x, cos, sin_signed):
    half = ROPE_AXIS_DIM // 2
    width = x.shape[1]
    lane = lax.broadcasted_iota(jnp.int32, x.shape, 1)
    low_half = (lane & (ROPE_AXIS_DIM - 1)) < half
    partner = jnp.where(low_half, pltpu.roll(x, width - half, axis=1), pltpu.roll(x, half, axis=1))
    return x * cos + partner * sin_signed


def _lat_attn_kernel(sink_ref, q_ref, k_ref, v_ref, ck_ref, cv_ref, cos_ref, sin_ref, o_ref, kr_ref, vb_ref):
    h = pl.program_id(1)
    L = k_ref.shape[1]
    nb = L // BLOCK
    rows4 = GQA * BLOCK
    c2 = SCALE * math.log2(math.e)
    kr_ref[...] = _rope(k_ref[0], cos_ref[:, :HEAD_DIM], sin_ref[:, :HEAD_DIM]).astype(BF16)
    vb_ref[...] = v_ref[0].astype(BF16)
    ck = ck_ref[0, 0].astype(BF16)
    cv = cv_ref[0, 0].astype(BF16)
    nt = (((1,), (1,)), ((), ()))

    r = lax.broadcasted_iota(jnp.int32, (rows4, 1), 0)
    sink = jnp.full((rows4, 1), sink_ref[h * GQA], F32)
    for g in range(1, GQA):
        sink = jnp.where(r >= g * BLOCK, sink_ref[h * GQA + g], sink)
    sink = sink * (1.0 / SCALE)
    a = r & (BLOCK - 1)
    b = lax.broadcasted_iota(jnp.int32, (1, BLOCK), 1)
    keep_prev = b >= a
    keep_next = b <= a

    for n in range(nb):
        rows = slice(n * BLOCK, (n + 1) * BLOCK)
        qa = _rope(q_ref[0, rows, :], cos_ref[rows, :], sin_ref[rows, :]).astype(BF16)
        q = jnp.concatenate([qa[:, g * HEAD_DIM:(g + 1) * HEAD_DIM] for g in range(GQA)], axis=0)
        scores, values = [], []
        for kb, keep in ((n - 1, keep_prev), (n, None), (n + 1, keep_next)):
            if 0 <= kb < nb:
                krows = slice(kb * BLOCK, (kb + 1) * BLOCK)
                s = lax.dot_general(q, kr_ref[krows, :], nt, preferred_element_type=F32)
                scores.append(s if keep is None else jnp.where(keep, s, -1e30))
                values.append(vb_ref[krows, :])
        scores.append(lax.dot_general(q, ck, nt, preferred_element_type=F32))
        values.append(cv)
        m = sink
        for s in scores:
            m = jnp.maximum(m, jnp.max(s, axis=-1, keepdims=True))
        ps = [jnp.exp2((s - m) * c2) for s in scores]
        den = jnp.exp2((sink - m) * c2)
        for p in ps:
            den = den + jnp.sum(p, axis=-1, keepdims=True)
        inv = 1.0 / den
        o = None
        for p, v in zip(ps, values):
            t = _bdot((p * inv).astype(BF16), v)
            o = t if o is None else o + t
        for g in range(GQA):
            o_ref[0, rows, g * HEAD_DIM:(g + 1) * HEAD_DIM] = o[g * BLOCK:(g + 1) * BLOCK].astype(BF16)


def _lat_attn_call(u3, cache_k_l, cache_v_l, layer, sink, cos, sin_signed):
    B, L, _ = u3.shape
    P = cache_k_l.shape[2]
    gw = GQA * HEAD_DIM
    c_spec = pl.BlockSpec((1, 1, P, HEAD_DIM), lambda b, h: (b, layer, 0, h))
    return pl.pallas_call(
        _lat_attn_kernel,
        grid=(B, N_KV_HEADS),
        in_specs=[
            pl.BlockSpec(memory_space=pltpu.SMEM),
            pl.BlockSpec((1, L, gw), lambda b, h: (b, 0, Q_OFF // gw + h)),
            pl.BlockSpec((1, L, HEAD_DIM), lambda b, h: (b, 0, K_OFF // HEAD_DIM + h)),
            pl.BlockSpec((1, L, HEAD_DIM), lambda b, h: (b, 0, V_OFF // HEAD_DIM + h)),
            c_spec, c_spec,
            _const_spec((L, gw)), _const_spec((L, gw)),
        ],
        out_specs=pl.BlockSpec((1, L, gw), lambda b, h: (b, 0, h)),
        out_shape=jax.ShapeDtypeStruct((B, L, ATT_WIDTH), BF16),
        scratch_shapes=[pltpu.VMEM((L, HEAD_DIM), BF16), pltpu.VMEM((L, HEAD_DIM), BF16)],
        compiler_params=_cparams(("arbitrary", "arbitrary")),
        name="latent_attention",
    )(sink, u3, u3, u3, cache_k_l, cache_v_l, jnp.tile(cos, (1, GQA)), jnp.tile(sin_signed, (1, GQA)))


def _rope_tables(L):
    rows = L // GRID_W
    row_pos = jnp.repeat(jnp.arange(rows), GRID_W)
    col_pos = jnp.arange(L) % GRID_W
    half = ROPE_AXIS_DIM // 2
    inv = ROPE_THETA ** (-jnp.arange(half, dtype=F32) * 2.0 / ROPE_AXIS_DIM)
    cos_parts, sin_parts = [], []
    for pos in (row_pos, col_pos):
        ang = pos.astype(F32)[:, None] * inv[None, :]
        cos_parts += [jnp.cos(ang), jnp.cos(ang)]
        sin_parts += [-jnp.sin(ang), jnp.sin(ang)]
    return jnp.concatenate(cos_parts, axis=-1), jnp.concatenate(sin_parts, axis=-1)


def _outproj_kernel(f_ref, hy_ref, att_ref, wf_ref, wh_ref, wa_ref, o_ref):
    o_ref[...] = (_bdot(f_ref[...], wf_ref[...]) + _bdot(hy_ref[...], wh_ref[...])
                  + _bdot(att_ref[...], wa_ref[...]))


def _outproj_call(f, hy, att, w_out, layer):
    T = f.shape[0]
    assert F_WIDTH == HY_WIDTH and ATT_WIDTH == F_WIDTH + HY_WIDTH
    return pl.pallas_call(
        _outproj_kernel,
        grid=(T // TM_MM, D_MODEL // TN_MM),
        in_specs=[
            pl.BlockSpec((TM_MM, F_WIDTH), lambda i, j: (i, 0)),
            pl.BlockSpec((TM_MM, HY_WIDTH), lambda i, j: (i, 0)),
            pl.BlockSpec((TM_MM, ATT_WIDTH), lambda i, j: (i, 0)),
            pl.BlockSpec((None, F_WIDTH, TN_MM), lambda i, j: (layer, 0, j)),
            pl.BlockSpec((None, HY_WIDTH, TN_MM), lambda i, j: (layer, 1, j)),
            pl.BlockSpec((None, ATT_WIDTH, TN_MM), lambda i, j: (layer, 1, j)),
        ],
        out_specs=pl.BlockSpec((TM_MM, TN_MM), lambda i, j: (i, j)),
        out_shape=jax.ShapeDtypeStruct((T, D_MODEL), F32),
        compiler_params=_cparams(("arbitrary", "arbitrary")),
        name="outproj",
    )(f, hy, att, w_out, w_out, w_out)


def _ffn_kernel(seq_len, nf, h_ref, wg_ref, wv_ref, cwg_ref, cwv_ref, cbg_ref, cbv_ref, wd_ref, o_ref,
                u_ref, act_ref):
    g = pl.program_id(0)
    tm = h_ref.shape[0]
    seg = min(seq_len, tm)
    n_seg = tm // seg
    base = [SUBLANE + s * (seg + SUBLANE) for s in range(n_seg)]

    @pl.when(g == 0)
    def _():
        u_ref[...] = jnp.zeros(u_ref.shape, F32)
        act_ref[...] = jnp.zeros(act_ref.shape, BF16)

    @pl.when((g == 0) | (lax.rem(g + (nf - 2), nf) == 0))
    def _():
        o_ref[...] = jnp.zeros(o_ref.shape, F32)

    live = (g > 0).astype(F32)

    def conv_piece(slot, p):
        rp = tm // FFN_PIECES
        s, r0 = divmod(p * rp, seg)
        lo = base[s] + r0

        def conv(part, cw_ref, cb_ref):
            win = lambda off: u_ref[1 - slot, part, lo + off:lo + off + rp, :]
            return win(-1) * cw_ref[0:1, :] + win(0) * cw_ref[1:2, :] + win(1) * cw_ref[2:3, :] + cb_ref[...]

        gate = conv(0, cwg_ref, cbg_ref)
        val = conv(1, cwv_ref, cbv_ref)
        act_ref[1 - slot, p * rp:(p + 1) * rp, :] = (_silu(gate) * (val * live)).astype(BF16)

    def down_cols(slot, p):
        cp = D_MODEL // FFN_PIECES
        cols = slice(p * cp, (p + 1) * cp)
        o_ref[:, cols] += _bdot(act_ref[slot], wd_ref[:, cols])

    def up_rows(slot, part, r):
        w_ref = (wg_ref, wv_ref)[part]
        s, r0 = divmod(r * up_rb, seg)
        u_ref[slot, part, base[s] + r0:base[s] + r0 + up_rb, :] = _bdot(
            h_ref[r * up_rb:(r + 1) * up_rb, :], w_ref[...])

    up_rb = min(seg, FFN_UP_ROWS)
    segments = [(functools.partial(down_cols, p=p), tm * (D_MODEL // FFN_PIECES)) for p in range(FFN_PIECES)]
    segments += [(functools.partial(up_rows, part=part, r=r), up_rb * D_MODEL)
                 for part in (0, 1) for r in range(tm // up_rb)]
    total_time = sum(t for _, t in segments[:-1])

    def step(slot):
        done, emitted = 0, 0
        for run, t in segments:
            while emitted < FFN_PIECES and emitted * total_time <= done * FFN_PIECES:
                conv_piece(slot, emitted)
                emitted += 1
            run(slot)
            done += t
        for p in range(emitted, FFN_PIECES):
            conv_piece(slot, p)

    for slot in (0, 1):
        pl.when((g & 1) == slot)(functools.partial(step, slot))


def _ffn_call(h, seq_len, w_up, conv_w, conv_b, w_down, layer):
    T = h.shape[0]
    nf = D_FF // TF_FFN
    n_tiles = T // TM_FFN
    assert TM_FFN % seq_len == 0
    cb = conv_b.reshape(DEPTH, 1, -1)
    wspec = lambda shape, imap: pl.BlockSpec((None,) + shape, imap)
    seg = min(seq_len, TM_FFN)
    u_rows = SUBLANE + (TM_FFN // seg) * (seg + SUBLANE)
    up_chunk = lambda g: lax.rem(g, nf)
    cv_chunk = lambda g: lax.rem(g + (nf - 1), nf)
    dn_chunk = lambda g: lax.rem(g + (nf - 2), nf)
    up_tile = lambda g: jnp.minimum(g // nf, n_tiles - 1)
    dn_tile = lambda g: jnp.clip((g + (nf - 2)) // nf - 1, 0, n_tiles - 1)
    return pl.pallas_call(
        functools.partial(_ffn_kernel, seq_len, nf),
        grid=(n_tiles * nf + 2,),
        in_specs=[
            pl.BlockSpec((TM_FFN, D_MODEL), lambda g: (up_tile(g), 0), pipeline_mode=pl.Buffered(1)),
            wspec((D_MODEL, TF_FFN), lambda g: (layer, 0, up_chunk(g))),
            wspec((D_MODEL, TF_FFN), lambda g: (layer, 0, nf + up_chunk(g))),
            wspec((3, TF_FFN), lambda g: (layer, 0, cv_chunk(g))),
            wspec((3, TF_FFN), lambda g: (layer, 0, nf + cv_chunk(g))),
            wspec((1, TF_FFN), lambda g: (layer, 0, cv_chunk(g))),
            wspec((1, TF_FFN), lambda g: (layer, 0, nf + cv_chunk(g))),
            wspec((TF_FFN, D_MODEL), lambda g: (layer, dn_chunk(g), 0)),
        ],
        out_specs=pl.BlockSpec((TM_FFN, D_MODEL), lambda g: (dn_tile(g), 0), pipeline_mode=pl.Buffered(1)),
        out_shape=jax.ShapeDtypeStruct((T, D_MODEL), F32),
        scratch_shapes=[pltpu.VMEM((2, 2, u_rows, TF_FFN), F32), pltpu.VMEM((2, TM_FFN, TF_FFN), BF16)],
        compiler_params=_cparams(("arbitrary",)),
        name="conv_ffn",
    )(h, w_up, w_up, conv_w, conv_w, cb, cb, w_down)


def _layer(x, h, B, L, layer, mods, modrow_of_tile, big, p, attend):
    T = B * L
    u3 = _inproj_call(h, big["w_in"], layer).reshape(B, L, IN_WIDTH)
    f = _fourier_call(u3, L)
    spectrum = _filters_call(L, p["flt_w1"], p["flt_b1"], p["flt_w2"], p["flt_b2"], p["flt_w3"], p["flt_freq"])
    hy = _hyena_call(u3, L, p["hy_conv_w"], p["hy_conv_b"], spectrum, p["hy_skip"])
    att = attend(u3)
    proj = _outproj_call(f.reshape(T, F_WIDTH), hy.reshape(T, HY_WIDTH), att.reshape(T, ATT_WIDTH),
                         big["w_out"], layer)
    x, h2 = _resid_ln_call(x, proj, mods, layer, 2, big["ln1_g"], big["ln1_b"], modrow_of_tile,
                           h_layer=layer, h_row=3)
    ff = _ffn_call(h2, L, big["ffn_w_up"], big["ffn_conv_w"], big["ffn_conv_b"], big["ffn_w_down"], layer)
    if layer + 1 < DEPTH:
        x, h_next = _resid_ln_call(x, ff, mods, layer, 5, big["ln2_g"], big["ln2_b"], modrow_of_tile,
                                   h_layer=layer + 1, h_row=0)
    else:
        x, h_next = _resid_ln_call(x, ff, mods, layer, 5, big["ln2_g"], big["ln2_b"], modrow_of_tile)
    return x, h_next, u3


def kernel(x_prompt, x_sample, cache_k, cache_v, c, c_ctx, w_mod, b_mod, w_in, w_out, attn_sink,
           hy_conv_w, hy_conv_b, flt_w1, flt_b1, flt_w2, flt_b2, flt_w3, flt_freq, hy_skip,
           ln1_g, ln1_b, ffn_w_up, ffn_conv_w, ffn_conv_b, ffn_w_down, ln2_g, ln2_b):
    Bp, S, _ = x_prompt.shape
    Bs, Ls, _ = x_sample.shape
    past = cache_k.shape[2]

    cvec = jnp.zeros((MOD_ROWS, D_MODEL), F32).at[0].set(c_ctx).at[1:1 + Bs].set(c)
    mods = _mod_call(cvec, w_mod, b_mod).reshape(DEPTH, MOD_ROWS, 6, D_MODEL)

    big = dict(w_in=w_in.astype(BF16), w_out=w_out.astype(BF16), ffn_w_up=ffn_w_up.astype(BF16),
               ffn_w_down=ffn_w_down.astype(BF16), ffn_conv_w=ffn_conv_w, ffn_conv_b=ffn_conv_b,
               ln1_g=ln1_g, ln1_b=ln1_b, ln2_g=ln2_g, ln2_b=ln2_b)
    ck = cache_k.reshape(Bs, DEPTH, past, KV_WIDTH)
    cv = cache_v.reshape(Bs, DEPTH, past, KV_WIDTH)
    cos, sin_signed = _rope_tables(Ls)

    ctx_row = lambda i, tm: 0
    lat_row = lambda i, tm: 1 + i * tm // Ls

    xp = x_prompt.reshape(Bp * S, D_MODEL)
    xs = x_sample.reshape(Bs * Ls, D_MODEL)
    hp = _ln_mod_call(xp, mods, 0, ctx_row)
    hs = _ln_mod_call(xs, mods, 0, lat_row)
    ctx_k, ctx_v = [], []
    for l in range(DEPTH):
        p = dict(flt_w1=flt_w1[l], flt_b1=flt_b1[l], flt_w2=flt_w2[l], flt_b2=flt_b2[l], flt_w3=flt_w3[l],
                 flt_freq=flt_freq[l], hy_conv_w=hy_conv_w[l], hy_conv_b=hy_conv_b[l], hy_skip=hy_skip[l])
        sink = attn_sink[l]
        xp, hp, up3 = _layer(xp, hp, Bp, S, l, mods, ctx_row, big, p, lambda u3: _ctx_attn_call(u3, sink))
        ctx_k.append(up3[:, :, K_OFF:K_OFF + KV_WIDTH].reshape(Bp, S, N_KV_HEADS, HEAD_DIM))
        ctx_v.append(up3[:, :, V_OFF:V_OFF + KV_WIDTH].reshape(Bp, S, N_KV_HEADS, HEAD_DIM))
        xs, hs, _ = _layer(xs, hs, Bs, Ls, l, mods, lat_row, big, p,
                           lambda u3: _lat_attn_call(u3, ck, cv, l, sink, cos, sin_signed))

    return (xp.reshape(Bp, S, D_MODEL), xs.reshape(Bs, Ls, D_MODEL),
            jnp.stack(ctx_k, axis=1), jnp.stack(ctx_v, axis=1))
```

```python
import functools
import math

import ml_dtypes
import numpy as np
import jax
import jax.numpy as jnp
from jax import lax
from jax.experimental import pallas as pl
from jax.experimental.pallas import tpu as pltpu

F32 = jnp.float32
BF16 = jnp.bfloat16

D_MODEL = 4096
DEPTH = 2
GRID_W = 64
F_WIDTH = D_MODEL // 4
F_GROUPS = 8
F_GROUP_W = F_WIDTH // F_GROUPS
HY_WIDTH = D_MODEL // 4
HY_ORDER = 2
N_BANDS = 16
POS_EMB = 1 + 2 * N_BANDS
FILTER_HID = 64
DECAY_TARGET = 1e-2
MIN_DECAY = math.log(DECAY_TARGET) / 1.5
MAX_DECAY = math.log(DECAY_TARGET) / 0.3
HEAD_DIM = 128
N_Q_HEADS = (D_MODEL // 2) // HEAD_DIM
N_KV_HEADS = N_Q_HEADS // 4
GQA = N_Q_HEADS // N_KV_HEADS
ATT_WIDTH = N_Q_HEADS * HEAD_DIM
KV_WIDTH = N_KV_HEADS * HEAD_DIM
WINDOW = 128
BLOCK = 128
SCALE = HEAD_DIM ** -0.5
ROPE_THETA = 10000.0
ROPE_AXIS_DIM = HEAD_DIM // 2
D_FF = 256 * ((8 * D_MODEL // 3 + 255) // 256)
LN_EPS = 1e-5
ALPHA = (2 * DEPTH) ** 0.25
IN_WIDTH = F_WIDTH + 3 * HY_WIDTH + ATT_WIDTH + 2 * KV_WIDTH
MIX_WIDTH = F_WIDTH + HY_WIDTH + ATT_WIDTH

HY_OFF = F_WIDTH
Q_OFF = F_WIDTH + 3 * HY_WIDTH
K_OFF = Q_OFF + ATT_WIDTH
V_OFF = K_OFF + KV_WIDTH

MOD_ROWS = 16
LANE = 128
SUBLANE = 8
VMEM_LIMIT = 56 * 1024 * 1024

BF16_ROWS = 16
TM_MM = 1024
TN_MM = 512
TM_LN = 256
TM_FFN = 1024
TF_FFN = 256
FFN_UP_ROWS = 256
FFN_PIECES = 16
HY_CT = 256
TN_MOD = 512
TK_MOD = 512
ROW_CHUNK = 32


def _cparams(sem):
    return pltpu.CompilerParams(dimension_semantics=sem, vmem_limit_bytes=VMEM_LIMIT)


def _const_spec(shape):
    return pl.BlockSpec(shape, lambda *_: (0,) * len(shape), pipeline_mode=pl.Buffered(1))


def _bdot(a, b):
    return jnp.dot(a, b, preferred_element_type=F32)


def _ln_rows(x):
    mu = jnp.mean(x, axis=-1, keepdims=True)
    xc = x - mu
    var = jnp.mean(xc * xc, axis=-1, keepdims=True)
    return xc * lax.rsqrt(var + LN_EPS)


def _silu(x):
    return x / (1.0 + jnp.exp(-x))


def _for_row_chunks(n_rows, body):
    def step(c, carry):
        body(pl.ds(pl.multiple_of(c * ROW_CHUNK, ROW_CHUNK), ROW_CHUNK))
        return carry
    lax.fori_loop(0, n_rows // ROW_CHUNK, step, 0)


def _hi_lo(x):
    hi = x.astype(ml_dtypes.bfloat16)
    lo = (x - hi.astype(np.float64)).astype(ml_dtypes.bfloat16)
    return jnp.asarray(hi), jnp.asarray(lo)


@functools.lru_cache(maxsize=None)
def _dft_mats(L):
    H = L // 2
    k = np.arange(H)
    t = np.arange(H)
    ang_e = 2.0 * np.pi * ((k[:, None] * t[None, :]) % L) / L
    fe = np.concatenate([np.cos(ang_e), -np.sin(ang_e)], axis=0)
    ang_o = np.pi * ((k[:, None] * (2 * t[None, :] + 1)) % (2 * L)) / L
    fo = np.concatenate([np.cos(ang_o), -np.sin(ang_o)], axis=0)
    bins = np.concatenate([k, L - k])
    n = np.arange(L)
    ang_s = np.pi * ((bins[:, None] * n[None, :]) % (2 * L)) / L
    wgt = np.where(np.concatenate([k, k]) == 0, 1.0, 2.0)[:, None] / (2 * L)
    fsel = np.concatenate([np.cos(ang_s) * wgt, -np.sin(ang_s) * wgt], axis=0)
    ang2 = 2.0 * np.pi * ((n[:, None] * n[None, :]) % L) / L
    fm = np.concatenate([np.cos(ang2), -np.sin(ang2)], axis=1) / np.sqrt(L)
    return dict(fe=_hi_lo(fe), fo=_hi_lo(fo), fe_t=_hi_lo(fe.T), fo_t=_hi_lo(fo.T), fsel=_hi_lo(fsel),
                fm=_hi_lo(fm))


@functools.lru_cache(maxsize=None)
def _chan_dft():
    n = np.arange(F_GROUP_W)
    ang = 2.0 * np.pi * ((n[:, None] * n[None, :]) % F_GROUP_W) / F_GROUP_W
    return _hi_lo(np.concatenate([np.cos(ang), np.sin(ang)], axis=1) / np.sqrt(F_GROUP_W))


def _mod_kernel(c_ref, w_ref, b_ref, o_ref):
    acc = b_ref[0] + jnp.zeros((MOD_ROWS, TN_MOD), F32)
    for k in range(0, D_MODEL, TK_MOD):
        s = _silu(c_ref[:, k:k + TK_MOD]).astype(BF16)
        acc = acc + _bdot(s, w_ref[0, k:k + TK_MOD, :].astype(BF16))
    o_ref[0] = acc


def _mod_call(cvec, w_mod, b_mod):
    n_out = w_mod.shape[-1]
    return pl.pallas_call(
        _mod_kernel,
        grid=(DEPTH, n_out // TN_MOD),
        in_specs=[
            pl.BlockSpec((MOD_ROWS, D_MODEL), lambda l, j: (0, 0)),
            pl.BlockSpec((1, D_MODEL, TN_MOD), lambda l, j: (l, 0, j)),
            pl.BlockSpec((1, 1, TN_MOD), lambda l, j: (l, 0, j)),
        ],
        out_specs=pl.BlockSpec((1, MOD_ROWS, TN_MOD), lambda l, j: (l, 0, j)),
        out_shape=jax.ShapeDtypeStruct((DEPTH, MOD_ROWS, n_out), F32),
        compiler_params=_cparams(("arbitrary", "arbitrary")),
        name="adaln_mod",
    )(cvec, w_mod, b_mod.reshape(DEPTH, 1, n_out))


def _inproj_kernel(h_ref, w_ref, o_ref):
    o_ref[...] = _bdot(h_ref[...], w_ref[...])


def _inproj_call(h, w_in, layer):
    T = h.shape[0]
    return pl.pallas_call(
        _inproj_kernel,
        grid=(T // TM_MM, IN_WIDTH // TN_MM),
        in_specs=[
            pl.BlockSpec((TM_MM, D_MODEL), lambda i, j: (i, 0)),
            pl.BlockSpec((None, D_MODEL, TN_MM), lambda i, j: (layer, 0, j)),
        ],
        out_specs=pl.BlockSpec((TM_MM, TN_MM), lambda i, j: (i, j)),
        out_shape=jax.ShapeDtypeStruct((T, IN_WIDTH), F32),
        compiler_params=_cparams(("arbitrary", "arbitrary")),
        name="inproj",
    )(h, w_in)


def _mod_spec(layer, modrow_of_tile, tm):
    return pl.BlockSpec((None, None, 6, D_MODEL), lambda i: (layer, modrow_of_tile(i, tm), 0, 0))


def _ln_mod_kernel(x_ref, mod_ref, h_ref):
    def body(rows):
        h = _ln_rows(x_ref[rows, :]) * (1.0 + mod_ref[1:2, :]) + mod_ref[0:1, :]
        h_ref[rows, :] = h.astype(BF16)
    _for_row_chunks(TM_LN, body)


def _ln_mod_call(x, mods, layer, modrow_of_tile):
    T = x.shape[0]
    return pl.pallas_call(
        _ln_mod_kernel,
        grid=(T // TM_LN,),
        in_specs=[pl.BlockSpec((TM_LN, D_MODEL), lambda i: (i, 0)), _mod_spec(layer, modrow_of_tile, TM_LN)],
        out_specs=pl.BlockSpec((TM_LN, D_MODEL), lambda i: (i, 0)),
        out_shape=jax.ShapeDtypeStruct((T, D_MODEL), BF16),
        compiler_params=_cparams(("arbitrary",)),
        name="ln_modulate",
    )(x, mods)


def _resid_ln_kernel(gate_row, h_row, x_ref, d_ref, modg_ref, g_ref, b_ref, *rest):
    if h_row is None:
        (xo_ref,) = rest
    else:
        modh_ref, xo_ref, ho_ref = rest

    def body(rows):
        y = ALPHA * x_ref[rows, :] + modg_ref[gate_row:gate_row + 1, :] * d_ref[rows, :]
        xn = _ln_rows(y) * g_ref[...] + b_ref[...]
        xo_ref[rows, :] = xn
        if h_row is not None:
            h = _ln_rows(xn) * (1.0 + modh_ref[h_row + 1:h_row + 2, :]) + modh_ref[h_row:h_row + 1, :]
            ho_ref[rows, :] = h.astype(BF16)
    _for_row_chunks(TM_LN, body)


def _resid_ln_call(x, d, mods, layer, gate_row, ln_g, ln_b, modrow_of_tile, h_layer=None, h_row=None):
    T = x.shape[0]
    tile = pl.BlockSpec((TM_LN, D_MODEL), lambda i: (i, 0))
    vec = pl.BlockSpec((None, 1, D_MODEL), lambda i: (layer, 0, 0))
    in_specs = [tile, tile, _mod_spec(layer, modrow_of_tile, TM_LN), vec, vec]
    args = [x, d, mods, ln_g.reshape(DEPTH, 1, D_MODEL), ln_b.reshape(DEPTH, 1, D_MODEL)]
    out_specs = [tile]
    out_shape = [jax.ShapeDtypeStruct((T, D_MODEL), F32)]
    if h_row is not None:
        in_specs.append(_mod_spec(h_layer, modrow_of_tile, TM_LN))
        args.append(mods)
        out_specs.append(tile)
        out_shape.append(jax.ShapeDtypeStruct((T, D_MODEL), BF16))
    out = pl.pallas_call(
        functools.partial(_resid_ln_kernel, gate_row, h_row),
        grid=(T // TM_LN,),
        in_specs=in_specs,
        out_specs=out_specs,
        out_shape=out_shape,
        compiler_params=_cparams(("arbitrary",)),
        name="residual_ln",
    )(*args)
    return out if h_row is not None else (out[0], None)


def _fourier_kernel(u_ref, cc_hi_ref, cc_lo_ref, fm_hi_ref, fm_lo_ref, o_ref, p_ref):
    L = u_ref.shape[1]
    rc = min(L, 256)
    for g in range(F_GROUPS):
        cols = slice(g * F_GROUP_W, (g + 1) * F_GROUP_W)
        for r in range(0, L, rc):
            ug = u_ref[0, r:r + rc, cols].astype(BF16)
            a = _bdot(ug, cc_hi_ref[...]) + _bdot(ug, cc_lo_ref[...])
            p_ref[r:r + rc, cols] = a[:, :F_GROUP_W].astype(BF16)
            p_ref[L + r:L + r + rc, cols] = a[:, F_GROUP_W:].astype(BF16)
    for r in range(0, L, rc):
        y = _bdot(fm_hi_ref[r:r + rc, :], p_ref[...]) + _bdot(fm_lo_ref[r:r + rc, :], p_ref[...])
        o_ref[0, r:r + rc, :] = y.astype(BF16)


def _fourier_call(u3, L):
    B = u3.shape[0]
    fm_hi, fm_lo = _dft_mats(L)["fm"]
    cc_hi, cc_lo = _chan_dft()
    return pl.pallas_call(
        _fourier_kernel,
        grid=(B,),
        in_specs=[
            pl.BlockSpec((1, L, F_WIDTH), lambda b: (b, 0, 0)),
            _const_spec((F_GROUP_W, 2 * F_GROUP_W)),
            _const_spec((F_GROUP_W, 2 * F_GROUP_W)),
            _const_spec((L, 2 * L)),
            _const_spec((L, 2 * L)),
        ],
        out_specs=pl.BlockSpec((1, L, F_WIDTH), lambda b: (b, 0, 0)),
        out_shape=jax.ShapeDtypeStruct((B, L, F_WIDTH), BF16),
        scratch_shapes=[pltpu.VMEM((2 * L, F_WIDTH), BF16)],
        compiler_params=_cparams(("arbitrary",)),
        name="fourier_mix",
    )(u3, cc_hi, cc_lo, fm_hi, fm_lo)


def _filter_taps_kernel(feat_ref, w1_ref, b1_ref, w2_ref, b2_ref, freq_ref, w3f_ref, w3b_ref, decay_ref,
                        fwd_ref, bsh_ref):
    L = feat_ref.shape[0]
    hp = lax.Precision.HIGHEST
    feats = feat_ref[...]
    freq = freq_ref[...]
    h = jnp.sin(freq * (jnp.dot(feats, w1_ref[...], precision=hp, preferred_element_type=F32) + b1_ref[...]))
    h = jnp.sin(freq * (jnp.dot(h, w2_ref[...], precision=hp, preferred_element_type=F32) + b2_ref[...]))
    win = jnp.exp(-feats[:, 0:1] * decay_ref[...])
    fwd = jnp.dot(h, w3f_ref[...], precision=hp, preferred_element_type=F32) * win
    bwd = jnp.dot(h, w3b_ref[...], precision=hp, preferred_element_type=F32) * win
    row = lax.broadcasted_iota(jnp.int32, bwd.shape, 0)
    bsh = jnp.where(row == 0, 0.0, pltpu.roll(bwd, 1, axis=0))
    norm = jnp.sum(jnp.abs(fwd), axis=0, keepdims=True) + jnp.sum(jnp.abs(bsh), axis=0, keepdims=True)
    fwd_ref[...] = fwd / norm
    bsh_ref[...] = bsh / norm


def _filter_spectrum_kernel(fwd_ref, bsh_ref, f_hi_ref, f_lo_ref, gr_ref, gi_ref, gh_ref):
    L = fwd_ref.shape[0]
    fwd = fwd_ref[...]
    bsh = bsh_ref[...]
    both = jnp.concatenate([fwd, bsh], axis=1)
    b_hi = both.astype(BF16)
    b_lo = (both - b_hi.astype(F32)).astype(BF16)
    ct = fwd.shape[1]
    rc = min(L, 256)
    for r in range(0, L, rc):
        def spec(rows):
            return (_bdot(f_hi_ref[rows, :], b_hi) + _bdot(f_lo_ref[rows, :], b_hi)
                    + _bdot(f_hi_ref[rows, :], b_lo))
        c = spec(slice(r, r + rc))
        s = spec(slice(L + r, L + r + rc))
        gr_ref[r:r + rc, :] = c[:, :ct] + c[:, ct:]
        gi_ref[r:r + rc, :] = s[:, :ct] - s[:, ct:]
    q = lax.broadcasted_iota(jnp.int32, fwd.shape, 0) & 3
    c4 = jnp.where(q == 0, 1.0, jnp.where(q == 2, -1.0, 0.0))
    s4 = jnp.where(q == 1, 1.0, jnp.where(q == 3, -1.0, 0.0))
    gh_ref[0:1, :] = jnp.sum(c4 * (fwd + bsh), axis=0, keepdims=True) * (1.0 / L)
    gh_ref[1:2, :] = jnp.sum(s4 * (bsh - fwd), axis=0, keepdims=True) * (1.0 / L)


def _filters_call(L, w1, b1, w2, b2, w3, freq):
    t = np.arange(L, dtype=np.float32)
    t_norm = t / np.float32(max(L - 1, 1))
    w = np.float32(2.0 * math.pi) * t / np.float32(L)
    bands = np.linspace(1e-4, N_BANDS - 1, N_BANDS, dtype=np.float32)
    feats = np.zeros((L, LANE), np.float32)
    feats[:, 0] = t_norm
    feats[:, 1:1 + N_BANDS] = np.cos(w[:, None] * bands[None])
    feats[:, 1 + N_BANDS:POS_EMB] = np.sin(w[:, None] * bands[None])
    decay = np.abs(np.linspace(MIN_DECAY, MAX_DECAY, HY_WIDTH, dtype=np.float32))[None, :]
    w1p = jnp.zeros((LANE, FILTER_HID), F32).at[:POS_EMB].set(w1)
    nct = HY_WIDTH // HY_CT
    ocw = HY_ORDER * HY_WIDTH
    row = lambda v: v.reshape(1, -1)
    small = lambda shape: pl.BlockSpec(shape, lambda o, ct: (0, 0))
    fwd, bsh = pl.pallas_call(
        _filter_taps_kernel,
        grid=(HY_ORDER, nct),
        in_specs=[
            small((L, LANE)), small((LANE, FILTER_HID)), small((1, FILTER_HID)),
            small((FILTER_HID, FILTER_HID)), small((1, FILTER_HID)), small((1, FILTER_HID)),
            pl.BlockSpec((FILTER_HID, HY_CT), lambda o, ct: (0, o * 2 * nct + ct)),
            pl.BlockSpec((FILTER_HID, HY_CT), lambda o, ct: (0, o * 2 * nct + nct + ct)),
            pl.BlockSpec((1, HY_CT), lambda o, ct: (0, ct)),
        ],
        out_specs=[pl.BlockSpec((L, HY_CT), lambda o, ct: (0, o * nct + ct))] * 2,
        out_shape=[jax.ShapeDtypeStruct((L, ocw), F32)] * 2,
        compiler_params=_cparams(("arbitrary", "arbitrary")),
        name="hyena_filter_taps",
    )(jnp.asarray(feats), w1p, row(b1), w2, row(b2), row(freq), w3, w3, jnp.asarray(decay))
    f_hi, f_lo = _dft_mats(L)["fsel"]
    return pl.pallas_call(
        _filter_spectrum_kernel,
        grid=(ocw // HY_CT,),
        in_specs=[
            pl.BlockSpec((L, HY_CT), lambda j: (0, j)),
            pl.BlockSpec((L, HY_CT), lambda j: (0, j)),
            _const_spec((2 * L, L)),
            _const_spec((2 * L, L)),
        ],
        out_specs=[pl.BlockSpec((L, HY_CT), lambda j: (0, j)),
                   pl.BlockSpec((L, HY_CT), lambda j: (0, j)),
                   pl.BlockSpec((2, HY_CT), lambda j: (0, j))],
        out_shape=[jax.ShapeDtypeStruct((L, ocw), F32), jax.ShapeDtypeStruct((L, ocw), F32),
                   jax.ShapeDtypeStruct((2, ocw), F32)],
        compiler_params=_cparams(("arbitrary",)),
        name="hyena_filter_spectrum",
    )(fwd, bsh, f_hi, f_lo)


def _hyena_kernel(v_ref, x1_ref, x2_ref, wv_ref, w1_ref, w2_ref, bv_ref, b1_ref, b2_ref,
                  gr0_ref, gi0_ref, gr1_ref, gi1_ref, gh0_ref, gh1_ref, skip_ref,
                  fe_hi_ref, fe_lo_ref, fo_hi_ref, fo_lo_ref, ie_hi_ref, ie_lo_ref, io_hi_ref, io_lo_ref,
                  o_ref, spec_e_ref, spec_o_ref, out_ref):
    L = v_ref.shape[1]
    H = L // 2
    rc = min(H, 256)
    row = lax.broadcasted_iota(jnp.int32, (H, HY_CT), 0)
    first = row == 0
    last = row == H - 1
    alt = jnp.where((row & 1) == 0, 1.0, -1.0)

    n_half = HY_CT // LANE

    def conv3(x_ref, w_ref, b_ref):
        for c in range(n_half):
            out_ref[c] = x_ref[0, :, c * LANE:(c + 1) * LANE]
        xe = jnp.concatenate([out_ref[c, pl.ds(0, H, stride=2), :] for c in range(n_half)], axis=1)
        xo = jnp.concatenate([out_ref[c, pl.ds(1, H, stride=2), :] for c in range(n_half)], axis=1)
        xo_prev = jnp.where(first, 0.0, pltpu.roll(xo, 1, axis=0))
        xe_next = jnp.where(last, 0.0, pltpu.roll(xe, H - 1, axis=0))
        w0, w1, w2, b = w_ref[0:1, :], w_ref[1:2, :], w_ref[2:3, :], b_ref[...]
        return xo_prev * w0 + xe * w1 + xo * w2 + b, xe * w0 + xo * w1 + xe_next * w2 + b

    ze, zo = conv3(v_ref, wv_ref, bv_ref)
    gates = (conv3(x1_ref, w1_ref, b1_ref), conv3(x2_ref, w2_ref, b2_ref))
    filt = ((gr0_ref, gi0_ref, gh0_ref), (gr1_ref, gi1_ref, gh1_ref))
    two_pass = lambda hi_ref, lo_ref, rows, x: _bdot(hi_ref[rows, :], x) + _bdot(lo_ref[rows, :], x)
    for o in range(HY_ORDER):
        gr_ref, gi_ref, gh_ref = filt[o]
        zeb = ze.astype(BF16)
        zob = zo.astype(BF16)
        for r in range(0, H, rc):
            re_rows, im_rows = slice(r, r + rc), slice(H + r, H + r + rc)
            er = two_pass(fe_hi_ref, fe_lo_ref, re_rows, zeb)
            ei = two_pass(fe_hi_ref, fe_lo_ref, im_rows, zeb)
            orr = two_pass(fo_hi_ref, fo_lo_ref, re_rows, zob)
            oi = two_pass(fo_hi_ref, fo_lo_ref, im_rows, zob)
            zr, zi, zpr, zpi = er + orr, ei + oi, er - orr, oi - ei
            gr, gi, gpr, gpi = gr_ref[re_rows, :], gi_ref[re_rows, :], gr_ref[im_rows, :], gi_ref[im_rows, :]
            yr, yi = zr * gr - zi * gi, zr * gi + zi * gr
            ypr, ypi = zpr * gpr - zpi * gpi, zpr * gpi + zpi * gpr
            spec_e_ref[re_rows, :] = (yr + ypr).astype(BF16)
            spec_e_ref[im_rows, :] = (yi - ypi).astype(BF16)
            spec_o_ref[re_rows, :] = (yr - ypr).astype(BF16)
            spec_o_ref[im_rows, :] = (yi + ypi).astype(BF16)
        zhr = jnp.sum(alt * ze, axis=0, keepdims=True)
        zhi = -jnp.sum(alt * zo, axis=0, keepdims=True)
        ghr, ghi = gh_ref[0:1, :], gh_ref[1:2, :]
        yhr, yhi = zhr * ghr - zhi * ghi, zhr * ghi + zhi * ghr
        ye, yo = [], []
        for r in range(0, H, rc):
            rows = slice(r, r + rc)
            ye.append(two_pass(ie_hi_ref, ie_lo_ref, rows, spec_e_ref[...]))
            yo.append(two_pass(io_hi_ref, io_lo_ref, rows, spec_o_ref[...]))
        ye = jnp.concatenate(ye, axis=0) + alt * yhr
        yo = jnp.concatenate(yo, axis=0) - alt * yhi
        skip = skip_ref[o:o + 1, :]
        ze = gates[o][0] * (ye + skip * ze)
        zo = gates[o][1] * (yo + skip * zo)
    for c in range(n_half):
        lanes = slice(c * LANE, (c + 1) * LANE)
        out_ref[c, pl.ds(0, H, stride=2), :] = ze[:, lanes]
        out_ref[c, pl.ds(1, H, stride=2), :] = zo[:, lanes]
        o_ref[0, :, lanes] = out_ref[c].astype(BF16)


def _hyena_call(u3, L, conv_w, conv_b, spectrum, skip):
    B = u3.shape[0]
    H = L // 2
    nct = HY_WIDTH // HY_CT
    hr, hi, hn = spectrum
    mats = _dft_mats(L)
    off = HY_OFF // HY_CT
    u_spec = lambda part: pl.BlockSpec((1, L, HY_CT), lambda b, ct: (b, 0, off + part * nct + ct))
    cw_spec = lambda part: pl.BlockSpec((3, HY_CT), lambda b, ct: (0, part * nct + ct))
    cb_spec = lambda part: pl.BlockSpec((1, HY_CT), lambda b, ct: (0, part * nct + ct))
    h_spec = lambda o: pl.BlockSpec((L, HY_CT), lambda b, ct: (0, o * nct + ct))
    n_spec = lambda o: pl.BlockSpec((2, HY_CT), lambda b, ct: (0, o * nct + ct))
    cb = conv_b.reshape(1, -1)
    return pl.pallas_call(
        _hyena_kernel,
        grid=(B, nct),
        in_specs=[
            u_spec(0), u_spec(1), u_spec(2),
            cw_spec(0), cw_spec(1), cw_spec(2),
            cb_spec(0), cb_spec(1), cb_spec(2),
            h_spec(0), h_spec(0), h_spec(1), h_spec(1), n_spec(0), n_spec(1),
            pl.BlockSpec((HY_ORDER, HY_CT), lambda b, ct: (0, ct)),
            _const_spec((L, H)), _const_spec((L, H)), _const_spec((L, H)), _const_spec((L, H)),
            _const_spec((H, L)), _const_spec((H, L)), _const_spec((H, L)), _const_spec((H, L)),
        ],
        out_specs=pl.BlockSpec((1, L, HY_CT), lambda b, ct: (b, 0, ct)),
        out_shape=jax.ShapeDtypeStruct((B, L, HY_WIDTH), BF16),
        scratch_shapes=[pltpu.VMEM((L, HY_CT), BF16), pltpu.VMEM((L, HY_CT), BF16),
                        pltpu.VMEM((HY_CT // LANE, L, LANE), F32)],
        compiler_params=_cparams(("arbitrary", "arbitrary")),
        name="hyena_mix",
    )(u3, u3, u3, conv_w, conv_w, conv_w, cb, cb, cb, hr, hi, hr, hi, hn, hn, skip,
      *mats["fe"], *mats["fo"], *mats["fe_t"], *mats["fo_t"])


def _sink_softmax_pv(scores, values, sink):
    m = sink
    for s in scores:
        m = jnp.maximum(m, jnp.max(s, axis=-1, keepdims=True))
    ps = [jnp.exp(s - m) for s in scores]
    den = jnp.exp(sink - m)
    for p in ps:
        den = den + jnp.sum(p, axis=-1, keepdims=True)
    inv = 1.0 / den
    out = None
    for p, v in zip(ps, values):
        t = _bdot((p * inv).astype(BF16), v)
        out = t if out is None else out + t
    return out


def _ctx_attn_kernel(sink_ref, q_ref, k_ref, v_ref, o_ref):
    h = pl.program_id(1)
    k = k_ref[0].astype(BF16)
    v = v_ref[0].astype(BF16)
    for g in range(GQA):
        cols = slice(g * HEAD_DIM, (g + 1) * HEAD_DIM)
        q = q_ref[0, :, cols].astype(BF16)
        s = lax.dot_general(q, k, (((1,), (1,)), ((), ())), preferred_element_type=F32) * SCALE
        o = _sink_softmax_pv([s], [v], sink_ref[h * GQA + g])
        o_ref[0, :, cols] = o.astype(BF16)


def _ctx_attn_call(u3, sink):
    B, S, _ = u3.shape
    gw = GQA * HEAD_DIM
    return pl.pallas_call(
        _ctx_attn_kernel,
        grid=(B, N_KV_HEADS),
        in_specs=[
            pl.BlockSpec(memory_space=pltpu.SMEM),
            pl.BlockSpec((1, S, gw), lambda b, h: (b, 0, Q_OFF // gw + h)),
            pl.BlockSpec((1, S, HEAD_DIM), lambda b, h: (b, 0, K_OFF // HEAD_DIM + h)),
            pl.BlockSpec((1, S, HEAD_DIM), lambda b, h: (b, 0, V_OFF // HEAD_DIM + h)),
        ],
        out_specs=pl.BlockSpec((1, S, gw), lambda b, h: (b, 0, h)),
        out_shape=jax.ShapeDtypeStruct((B, S, ATT_WIDTH), BF16),
        compiler_params=_cparams(("arbitrary", "arbitrary")),
        name="context_attention",
    )(sink, u3, u3, u3)


def _rope(x, cos, sin_signed):
    half = ROPE_AXIS_DIM // 2
    width = x.shape[1]
    lane = lax.broadcasted_iota(jnp.int32, x.shape, 1)
    low_half = (lane & (ROPE_AXIS_DIM - 1)) < half
    partner = jnp.where(low_half, pltpu.roll(x, width - half, axis=1), pltpu.roll(x, half, axis=1))
    return x * cos + partner * sin_signed


def _lat_attn_kernel(sink_ref, q_ref, k_ref, v_ref, ck_ref, cv_ref, cos_ref, sin_ref, o_ref, kr_ref, vb_ref):
    h = pl.program_id(1)
    L = k_ref.shape[1]
    nb = L // BLOCK
    rows4 = GQA * BLOCK
    c2 = SCALE * math.log2(math.e)
    kr_ref[...] = _rope(k_ref[0], cos_ref[:, :HEAD_DIM], sin_ref[:, :HEAD_DIM]).astype(BF16)
    vb_ref[...] = v_ref[0].astype(BF16)
    ck = ck_ref[0, 0].astype(BF16)
    cv = cv_ref[0, 0].astype(BF16)
    nt = (((1,), (1,)), ((), ()))

    r = lax.broadcasted_iota(jnp.int32, (rows4, 1), 0)
    sink = jnp.full((rows4, 1), sink_ref[h * GQA], F32)
    for g in range(1, GQA):
        sink = jnp.where(r >= g * BLOCK, sink_ref[h * GQA + g], sink)
    sink = sink * (1.0 / SCALE)
    a = r & (BLOCK - 1)
    b = lax.broadcasted_iota(jnp.int32, (1, BLOCK), 1)
    keep_prev = b >= a
    keep_next = b <= a

    for n in range(nb):
        rows = slice(n * BLOCK, (n + 1) * BLOCK)
        qa = _rope(q_ref[0, rows, :], cos_ref[rows, :], sin_ref[rows, :]).astype(BF16)
        q = jnp.concatenate([qa[:, g * HEAD_DIM:(g + 1) * HEAD_DIM] for g in range(GQA)], axis=0)
        scores, values = [], []
        for kb, keep in ((n - 1, keep_prev), (n, None), (n + 1, keep_next)):
            if 0 <= kb < nb:
                krows = slice(kb * BLOCK, (kb + 1) * BLOCK)
                s = lax.dot_general(q, kr_ref[krows, :], nt, preferred_element_type=F32)
                scores.append(s if keep is None else jnp.where(keep, s, -1e30))
                values.append(vb_ref[krows, :])
        scores.append(lax.dot_general(q, ck, nt, preferred_element_type=F32))
        values.append(cv)
        m = sink
        for s in scores:
            m = jnp.maximum(m, jnp.max(s, axis=-1, keepdims=True))
        ps = [jnp.exp2((s - m) * c2) for s in scores]
        den = jnp.exp2((sink - m) * c2)
        for p in ps:
            den = den + jnp.sum(p, axis=-1, keepdims=True)
        inv = 1.0 / den
        o = None
        for p, v in zip(ps, values):
            t = _bdot((p * inv).astype(BF16), v)
            o = t if o is None else o + t
        for g in range(GQA):
            o_ref[0, rows, g * HEAD_DIM:(g + 1) * HEAD_DIM] = o[g * BLOCK:(g + 1) * BLOCK].astype(BF16)


def _lat_attn_call(u3, cache_k_l, cache_v_l, layer, sink, cos, sin_signed):
    B, L, _ = u3.shape
    P = cache_k_l.shape[2]
    gw = GQA * HEAD_DIM
    c_spec = pl.BlockSpec((1, 1, P, HEAD_DIM), lambda b, h: (b, layer, 0, h))
    return pl.pallas_call(
        _lat_attn_kernel,
        grid=(B, N_KV_HEADS),
        in_specs=[
            pl.BlockSpec(memory_space=pltpu.SMEM),
            pl.BlockSpec((1, L, gw), lambda b, h: (b, 0, Q_OFF // gw + h)),
            pl.BlockSpec((1, L, HEAD_DIM), lambda b, h: (b, 0, K_OFF // HEAD_DIM + h)),
            pl.BlockSpec((1, L, HEAD_DIM), lambda b, h: (b, 0, V_OFF // HEAD_DIM + h)),
            c_spec, c_spec,
            _const_spec((L, gw)), _const_spec((L, gw)),
        ],
        out_specs=pl.BlockSpec((1, L, gw), lambda b, h: (b, 0, h)),
        out_shape=jax.ShapeDtypeStruct((B, L, ATT_WIDTH), BF16),
        scratch_shapes=[pltpu.VMEM((L, HEAD_DIM), BF16), pltpu.VMEM((L, HEAD_DIM), BF16)],
        compiler_params=_cparams(("arbitrary", "arbitrary")),
        name="latent_attention",
    )(sink, u3, u3, u3, cache_k_l, cache_v_l, jnp.tile(cos, (1, GQA)), jnp.tile(sin_signed, (1, GQA)))


def _rope_tables(L):
    rows = L // GRID_W
    row_pos = jnp.repeat(jnp.arange(rows), GRID_W)
    col_pos = jnp.arange(L) % GRID_W
    half = ROPE_AXIS_DIM // 2
    inv = ROPE_THETA ** (-jnp.arange(half, dtype=F32) * 2.0 / ROPE_AXIS_DIM)
    cos_parts, sin_parts = [], []
    for pos in (row_pos, col_pos):
        ang = pos.astype(F32)[:, None] * inv[None, :]
        cos_parts += [jnp.cos(ang), jnp.cos(ang)]
        sin_parts += [-jnp.sin(ang), jnp.sin(ang)]
    return jnp.concatenate(cos_parts, axis=-1), jnp.concatenate(sin_parts, axis=-1)


def _outproj_kernel(f_ref, hy_ref, att_ref, wf_ref, wh_ref, wa_ref, o_ref):
    o_ref[...] = (_bdot(f_ref[...], wf_ref[...]) + _bdot(hy_ref[...], wh_ref[...])
                  + _bdot(att_ref[...], wa_ref[...]))


def _outproj_call(f, hy, att, w_out, layer):
    T = f.shape[0]
    assert F_WIDTH == HY_WIDTH and ATT_WIDTH == F_WIDTH + HY_WIDTH
    return pl.pallas_call(
        _outproj_kernel,
        grid=(T // TM_MM, D_MODEL // TN_MM),
        in_specs=[
            pl.BlockSpec((TM_MM, F_WIDTH), lambda i, j: (i, 0)),
            pl.BlockSpec((TM_MM, HY_WIDTH), lambda i, j: (i, 0)),
            pl.BlockSpec((TM_MM, ATT_WIDTH), lambda i, j: (i, 0)),
            pl.BlockSpec((None, F_WIDTH, TN_MM), lambda i, j: (layer, 0, j)),
            pl.BlockSpec((None, HY_WIDTH, TN_MM), lambda i, j: (layer, 1, j)),
            pl.BlockSpec((None, ATT_WIDTH, TN_MM), lambda i, j: (layer, 1, j)),
        ],
        out_specs=pl.BlockSpec((TM_MM, TN_MM), lambda i, j: (i, j)),
        out_shape=jax.ShapeDtypeStruct((T, D_MODEL), F32),
        compiler_params=_cparams(("arbitrary", "arbitrary")),
        name="outproj",
    )(f, hy, att, w_out, w_out, w_out)


def _ffn_kernel(seq_len, nf, h_ref, wg_ref, wv_ref, cwg_ref, cwv_ref, cbg_ref, cbv_ref, wd_ref, o_ref,
                u_ref, act_ref):
    g = pl.program_id(0)
    tm = h_ref.shape[0]
    seg = min(seq_len, tm)
    n_seg = tm // seg
    base = [SUBLANE + s * (seg + SUBLANE) for s in range(n_seg)]

    @pl.when(g == 0)
    def _():
        u_ref[...] = jnp.zeros(u_ref.shape, F32)
        act_ref[...] = jnp.zeros(act_ref.shape, BF16)

    @pl.when((g == 0) | (lax.rem(g + (nf - 2), nf) == 0))
    def _():
        o_ref[...] = jnp.zeros(o_ref.shape, F32)

    live = (g > 0).astype(F32)

    def conv_piece(slot, p):
        rp = tm // FFN_PIECES
        s, r0 = divmod(p * rp, seg)
        lo = base[s] + r0

        def conv(part, cw_ref, cb_ref):
            win = lambda off: u_ref[1 - slot, part, lo + off:lo + off + rp, :]
            return win(-1) * cw_ref[0:1, :] + win(0) * cw_ref[1:2, :] + win(1) * cw_ref[2:3, :] + cb_ref[...]

        gate = conv(0, cwg_ref, cbg_ref)
        val = conv(1, cwv_ref, cbv_ref)
        act_ref[1 - slot, p * rp:(p + 1) * rp, :] = (_silu(gate) * (val * live)).astype(BF16)

    def down_cols(slot, p):
        cp = D_MODEL // FFN_PIECES
        cols = slice(p * cp, (p + 1) * cp)
        o_ref[:, cols] += _bdot(act_ref[slot], wd_ref[:, cols].astype(BF16))

    def up_rows(slot, part, r):
        w_ref = (wg_ref, wv_ref)[part]
        s, r0 = divmod(r * up_rb, seg)
        u_ref[slot, part, base[s] + r0:base[s] + r0 + up_rb, :] = _bdot(
            h_ref[r * up_rb:(r + 1) * up_rb, :], w_ref[...])

    up_rb = min(seg, FFN_UP_ROWS)
    segments = [(functools.partial(down_cols, p=p), tm * (D_MODEL // FFN_PIECES)) for p in range(FFN_PIECES)]
    segments += [(functools.partial(up_rows, part=part, r=r), up_rb * D_MODEL)
                 for part in (0, 1) for r in range(tm // up_rb)]
    total_time = sum(t for _, t in segments[:-1])

    def step(slot):
        done, emitted = 0, 0
        for run, t in segments:
            while emitted < FFN_PIECES and emitted * total_time <= done * FFN_PIECES:
                conv_piece(slot, emitted)
                emitted += 1
            run(slot)
            done += t
        for p in range(emitted, FFN_PIECES):
            conv_piece(slot, p)

    for slot in (0, 1):
        pl.when((g & 1) == slot)(functools.partial(step, slot))


def _ffn_call(h, seq_len, w_up, conv_w, conv_b, w_down, layer):
    T = h.shape[0]
    nf = D_FF // TF_FFN
    n_tiles = T // TM_FFN
    assert TM_FFN % seq_len == 0
    cb = conv_b.reshape(DEPTH, 1, -1)
    wspec = lambda shape, imap: pl.BlockSpec((None,) + shape, imap)
    seg = min(seq_len, TM_FFN)
    u_rows = SUBLANE + (TM_FFN // seg) * (seg + SUBLANE)
    up_chunk = lambda g: lax.rem(g, nf)
    cv_chunk = lambda g: lax.rem(g + (nf - 1), nf)
    dn_chunk = lambda g: lax.rem(g + (nf - 2), nf)
    up_tile = lambda g: jnp.minimum(g // nf, n_tiles - 1)
    dn_tile = lambda g: jnp.clip((g + (nf - 2)) // nf - 1, 0, n_tiles - 1)
    return pl.pallas_call(
        functools.partial(_ffn_kernel, seq_len, nf),
        grid=(n_tiles * nf + 2,),
        in_specs=[
            pl.BlockSpec((TM_FFN, D_MODEL), lambda g: (up_tile(g), 0), pipeline_mode=pl.Buffered(1)),
            wspec((D_MODEL, TF_FFN), lambda g: (layer, 0, up_chunk(g))),
            wspec((D_MODEL, TF_FFN), lambda g: (layer, 0, nf + up_chunk(g))),
            wspec((3, TF_FFN), lambda g: (layer, 0, cv_chunk(g))),
            wspec((3, TF_FFN), lambda g: (layer, 0, nf + cv_chunk(g))),
            wspec((1, TF_FFN), lambda g: (layer, 0, cv_chunk(g))),
            wspec((1, TF_FFN), lambda g: (layer, 0, nf + cv_chunk(g))),
            wspec((TF_FFN, D_MODEL), lambda g: (layer, dn_chunk(g), 0)),
        ],
        out_specs=pl.BlockSpec((TM_FFN, D_MODEL), lambda g: (dn_tile(g), 0), pipeline_mode=pl.Buffered(1)),
        out_shape=jax.ShapeDtypeStruct((T, D_MODEL), F32),
        scratch_shapes=[pltpu.VMEM((2, 2, u_rows, TF_FFN), F32), pltpu.VMEM((2, TM_FFN, TF_FFN), BF16)],
        compiler_params=_cparams(("arbitrary",)),
        name="conv_ffn",
    )(h, w_up, w_up, conv_w, conv_w, cb, cb, w_down)


def _layer(x, h, B, L, layer, mods, modrow_of_tile, big, p, attend):
    T = B * L
    u3 = _inproj_call(h, big["w_in"], layer).reshape(B, L, IN_WIDTH)
    f = _fourier_call(u3, L)
    spectrum = _filters_call(L, p["flt_w1"], p["flt_b1"], p["flt_w2"], p["flt_b2"], p["flt_w3"], p["flt_freq"])
    hy = _hyena_call(u3, L, p["hy_conv_w"], p["hy_conv_b"], spectrum, p["hy_skip"])
    att = attend(u3)
    proj = _outproj_call(f.reshape(T, F_WIDTH), hy.reshape(T, HY_WIDTH), att.reshape(T, ATT_WIDTH),
                         big["w_out"], layer)
    x, h2 = _resid_ln_call(x, proj, mods, layer, 2, big["ln1_g"], big["ln1_b"], modrow_of_tile,
                           h_layer=layer, h_row=3)
    ff = _ffn_call(h2, L, big["ffn_w_up"], big["ffn_conv_w"], big["ffn_conv_b"], big["ffn_w_down"], layer)
    if layer + 1 < DEPTH:
        x, h_next = _resid_ln_call(x, ff, mods, layer, 5, big["ln2_g"], big["ln2_b"], modrow_of_tile,
                                   h_layer=layer + 1, h_row=0)
    else:
        x, h_next = _resid_ln_call(x, ff, mods, layer, 5, big["ln2_g"], big["ln2_b"], modrow_of_tile)
    return x, h_next, u3


def kernel(x_prompt, x_sample, cache_k, cache_v, c, c_ctx, w_mod, b_mod, w_in, w_out, attn_sink,
           hy_conv_w, hy_conv_b, flt_w1, flt_b1, flt_w2, flt_b2, flt_w3, flt_freq, hy_skip,
           ln1_g, ln1_b, ffn_w_up, ffn_conv_w, ffn_conv_b, ffn_w_down, ln2_g, ln2_b):
    Bp, S, _ = x_prompt.shape
    Bs, Ls, _ = x_sample.shape
    past = cache_k.shape[2]

    cvec = jnp.zeros((MOD_ROWS, D_MODEL), F32).at[0].set(c_ctx).at[1:1 + Bs].set(c)
    mods = _mod_call(cvec, w_mod, b_mod).reshape(DEPTH, MOD_ROWS, 6, D_MODEL)

    big = dict(w_in=w_in.astype(BF16), w_out=w_out.astype(BF16), ffn_w_up=ffn_w_up.astype(BF16),
               ffn_w_down=ffn_w_down, ffn_conv_w=ffn_conv_w, ffn_conv_b=ffn_conv_b,
               ln1_g=ln1_g, ln1_b=ln1_b, ln2_g=ln2_g, ln2_b=ln2_b)
    ck = cache_k.reshape(Bs, DEPTH, past, KV_WIDTH)
    cv = cache_v.reshape(Bs, DEPTH, past, KV_WIDTH)
    cos, sin_signed = _rope_tables(Ls)

    ctx_row = lambda i, tm: 0
    lat_row = lambda i, tm: 1 + i * tm // Ls

    xp = x_prompt.reshape(Bp * S, D_MODEL)
    xs = x_sample.reshape(Bs * Ls, D_MODEL)
    hp = _ln_mod_call(xp, mods, 0, ctx_row)
    hs = _ln_mod_call(xs, mods, 0, lat_row)
    ctx_k, ctx_v = [], []
    for l in range(DEPTH):
        p = dict(flt_w1=flt_w1[l], flt_b1=flt_b1[l], flt_w2=flt_w2[l], flt_b2=flt_b2[l], flt_w3=flt_w3[l],
                 flt_freq=flt_freq[l], hy_conv_w=hy_conv_w[l], hy_conv_b=hy_conv_b[l], hy_skip=hy_skip[l])
        sink = attn_sink[l]
        xp, hp, up3 = _layer(xp, hp, Bp, S, l, mods, ctx_row, big, p, lambda u3: _ctx_attn_call(u3, sink))
        ctx_k.append(up3[:, :, K_OFF:K_OFF + KV_WIDTH].reshape(Bp, S, N_KV_HEADS, HEAD_DIM))
        ctx_v.append(up3[:, :, V_OFF:V_OFF + KV_WIDTH].reshape(Bp, S, N_KV_HEADS, HEAD_DIM))
        xs, hs, _ = _layer(xs, hs, Bs, Ls, l, mods, lat_row, big, p,
                           lambda u3: _lat_attn_call(u3, ck, cv, l, sink, cos, sin_signed))

    return (xp.reshape(Bp, S, D_MODEL), xs.reshape(Bs, Ls, D_MODEL),
            jnp.stack(ctx_k, axis=1), jnp.stack(ctx_v, axis=1))
```

```python
import functools
import math

import ml_dtypes
import numpy as np
import jax
import jax.numpy as jnp
from jax import lax
from jax.experimental import pallas as pl
from jax.experimental.pallas import tpu as pltpu

F32 = jnp.float32
BF16 = jnp.bfloat16

D_MODEL = 4096
DEPTH = 2
GRID_W = 64
F_WIDTH = D_MODEL // 4
F_GROUPS = 8
F_GROUP_W = F_WIDTH // F_GROUPS
HY_WIDTH = D_MODEL // 4
HY_ORDER = 2
N_BANDS = 16
POS_EMB = 1 + 2 * N_BANDS
FILTER_HID = 64
DECAY_TARGET = 1e-2
MIN_DECAY = math.log(DECAY_TARGET) / 1.5
MAX_DECAY = math.log(DECAY_TARGET) / 0.3
HEAD_DIM = 128
N_Q_HEADS = (D_MODEL // 2) // HEAD_DIM
N_KV_HEADS = N_Q_HEADS // 4
GQA = N_Q_HEADS // N_KV_HEADS
ATT_WIDTH = N_Q_HEADS * HEAD_DIM
KV_WIDTH = N_KV_HEADS * HEAD_DIM
WINDOW = 128
BLOCK = 128
SCALE = HEAD_DIM ** -0.5
ROPE_THETA = 10000.0
ROPE_AXIS_DIM = HEAD_DIM // 2
D_FF = 256 * ((8 * D_MODEL // 3 + 255) // 256)
LN_EPS = 1e-5
ALPHA = (2 * DEPTH) ** 0.25
IN_WIDTH = F_WIDTH + 3 * HY_WIDTH + ATT_WIDTH + 2 * KV_WIDTH
MIX_WIDTH = F_WIDTH + HY_WIDTH + ATT_WIDTH

HY_OFF = F_WIDTH
Q_OFF = F_WIDTH + 3 * HY_WIDTH
K_OFF = Q_OFF + ATT_WIDTH
V_OFF = K_OFF + KV_WIDTH

MOD_ROWS = 16
LANE = 128
SUBLANE = 8
VMEM_LIMIT = 56 * 1024 * 1024

BF16_ROWS = 16
TM_MM = 1024
TN_MM = 512
TM_LN = 256
TM_FFN = 1024
TF_FFN = 256
FFN_UP_ROWS = 256
FFN_PIECES = 16
FFN_CONV_PIECES = 16
HY_CT = 256
TN_MOD = 512
TK_MOD = 512
ROW_CHUNK = 32


def _cparams(sem):
    return pltpu.CompilerParams(dimension_semantics=sem, vmem_limit_bytes=VMEM_LIMIT)


def _const_spec(shape):
    return pl.BlockSpec(shape, lambda *_: (0,) * len(shape), pipeline_mode=pl.Buffered(1))


def _bdot(a, b):
    return jnp.dot(a, b, preferred_element_type=F32)


def _ln_rows(x):
    mu = jnp.mean(x, axis=-1, keepdims=True)
    xc = x - mu
    var = jnp.mean(xc * xc, axis=-1, keepdims=True)
    return xc * lax.rsqrt(var + LN_EPS)


def _silu(x):
    return x / (1.0 + jnp.exp(-x))


def _for_row_chunks(n_rows, body):
    def step(c, carry):
        body(pl.ds(pl.multiple_of(c * ROW_CHUNK, ROW_CHUNK), ROW_CHUNK))
        return carry
    lax.fori_loop(0, n_rows // ROW_CHUNK, step, 0)


def _hi_lo(x):
    hi = x.astype(ml_dtypes.bfloat16)
    lo = (x - hi.astype(np.float64)).astype(ml_dtypes.bfloat16)
    return jnp.asarray(hi), jnp.asarray(lo)


@functools.lru_cache(maxsize=None)
def _dft_mats(L):
    H = L // 2
    k = np.arange(H)
    t = np.arange(H)
    ang_e = 2.0 * np.pi * ((k[:, None] * t[None, :]) % L) / L
    fe = np.concatenate([np.cos(ang_e), -np.sin(ang_e)], axis=0)
    ang_o = np.pi * ((k[:, None] * (2 * t[None, :] + 1)) % (2 * L)) / L
    fo = np.concatenate([np.cos(ang_o), -np.sin(ang_o)], axis=0)
    bins = np.concatenate([k, L - k])
    n = np.arange(L)
    ang_s = np.pi * ((bins[:, None] * n[None, :]) % (2 * L)) / L
    wgt = np.where(np.concatenate([k, k]) == 0, 1.0, 2.0)[:, None] / (2 * L)
    fsel = np.concatenate([np.cos(ang_s) * wgt, -np.sin(ang_s) * wgt], axis=0)
    ang_me = 2.0 * np.pi * ((k[:, None] * (2 * t[None, :])) % L) / L
    ang_mo = 2.0 * np.pi * ((k[:, None] * (2 * t[None, :] + 1)) % L) / L
    fme = np.concatenate([np.cos(ang_me), -np.sin(ang_me)], axis=1) / np.sqrt(L)
    fmo = np.concatenate([np.cos(ang_mo), -np.sin(ang_mo)], axis=1) / np.sqrt(L)
    return dict(fe=_hi_lo(fe), fo=_hi_lo(fo), fe_t=_hi_lo(fe.T), fo_t=_hi_lo(fo.T), fsel=_hi_lo(fsel),
                fme=_hi_lo(fme), fmo=_hi_lo(fmo))


@functools.lru_cache(maxsize=None)
def _chan_dft():
    n = np.arange(F_GROUP_W)
    ang = 2.0 * np.pi * ((n[:, None] * n[None, :]) % F_GROUP_W) / F_GROUP_W
    return _hi_lo(np.concatenate([np.cos(ang), np.sin(ang)], axis=1) / np.sqrt(F_GROUP_W))


def _mod_kernel(c_ref, w_ref, b_ref, o_ref):
    acc = b_ref[0] + jnp.zeros((MOD_ROWS, TN_MOD), F32)
    for k in range(0, D_MODEL, TK_MOD):
        s = _silu(c_ref[:, k:k + TK_MOD]).astype(BF16)
        acc = acc + _bdot(s, w_ref[0, k:k + TK_MOD, :].astype(BF16))
    o_ref[0] = acc


def _mod_call(cvec, w_mod, b_mod):
    n_out = w_mod.shape[-1]
    return pl.pallas_call(
        _mod_kernel,
        grid=(DEPTH, n_out // TN_MOD),
        in_specs=[
            pl.BlockSpec((MOD_ROWS, D_MODEL), lambda l, j: (0, 0)),
            pl.BlockSpec((1, D_MODEL, TN_MOD), lambda l, j: (l, 0, j)),
            pl.BlockSpec((1, 1, TN_MOD), lambda l, j: (l, 0, j)),
        ],
        out_specs=pl.BlockSpec((1, MOD_ROWS, TN_MOD), lambda l, j: (l, 0, j)),
        out_shape=jax.ShapeDtypeStruct((DEPTH, MOD_ROWS, n_out), F32),
        compiler_params=_cparams(("arbitrary", "arbitrary")),
        name="adaln_mod",
    )(cvec, w_mod, b_mod.reshape(DEPTH, 1, n_out))


def _inproj_kernel(h_ref, w_ref, o_ref):
    o_ref[...] = _bdot(h_ref[...], w_ref[...])


def _inproj_call(h, w_in, layer):
    T = h.shape[0]
    return pl.pallas_call(
        _inproj_kernel,
        grid=(T // TM_MM, IN_WIDTH // TN_MM),
        in_specs=[
            pl.BlockSpec((TM_MM, D_MODEL), lambda i, j: (i, 0)),
            pl.BlockSpec((None, D_MODEL, TN_MM), lambda i, j: (layer, 0, j)),
        ],
        out_specs=pl.BlockSpec((TM_MM, TN_MM), lambda i, j: (i, j)),
        out_shape=jax.ShapeDtypeStruct((T, IN_WIDTH), F32),
        compiler_params=_cparams(("arbitrary", "arbitrary")),
        name="inproj",
    )(h, w_in)


def _mod_spec(layer, modrow_of_tile, tm):
    return pl.BlockSpec((None, None, 6, D_MODEL), lambda i: (layer, modrow_of_tile(i, tm), 0, 0))


def _ln_mod_kernel(x_ref, mod_ref, h_ref):
    def body(rows):
        h = _ln_rows(x_ref[rows, :]) * (1.0 + mod_ref[1:2, :]) + mod_ref[0:1, :]
        h_ref[rows, :] = h.astype(BF16)
    _for_row_chunks(TM_LN, body)


def _ln_mod_call(x, mods, layer, modrow_of_tile):
    T = x.shape[0]
    return pl.pallas_call(
        _ln_mod_kernel,
        grid=(T // TM_LN,),
        in_specs=[pl.BlockSpec((TM_LN, D_MODEL), lambda i: (i, 0)), _mod_spec(layer, modrow_of_tile, TM_LN)],
        out_specs=pl.BlockSpec((TM_LN, D_MODEL), lambda i: (i, 0)),
        out_shape=jax.ShapeDtypeStruct((T, D_MODEL), BF16),
        compiler_params=_cparams(("arbitrary",)),
        name="ln_modulate",
    )(x, mods)


def _resid_ln_kernel(gate_row, h_row, x_ref, d_ref, modg_ref, g_ref, b_ref, *rest):
    if h_row is None:
        (xo_ref,) = rest
    else:
        modh_ref, xo_ref, ho_ref = rest

    def body(rows):
        y = ALPHA * x_ref[rows, :] + modg_ref[gate_row:gate_row + 1, :] * d_ref[rows, :]
        xn = _ln_rows(y) * g_ref[...] + b_ref[...]
        xo_ref[rows, :] = xn
        if h_row is not None:
            h = _ln_rows(xn) * (1.0 + modh_ref[h_row + 1:h_row + 2, :]) + modh_ref[h_row:h_row + 1, :]
            ho_ref[rows, :] = h.astype(BF16)
    _for_row_chunks(TM_LN, body)


def _resid_ln_call(x, d, mods, layer, gate_row, ln_g, ln_b, modrow_of_tile, h_layer=None, h_row=None):
    T = x.shape[0]
    tile = pl.BlockSpec((TM_LN, D_MODEL), lambda i: (i, 0))
    vec = pl.BlockSpec((None, 1, D_MODEL), lambda i: (layer, 0, 0))
    in_specs = [tile, tile, _mod_spec(layer, modrow_of_tile, TM_LN), vec, vec]
    args = [x, d, mods, ln_g.reshape(DEPTH, 1, D_MODEL), ln_b.reshape(DEPTH, 1, D_MODEL)]
    out_specs = [tile]
    out_shape = [jax.ShapeDtypeStruct((T, D_MODEL), F32)]
    if h_row is not None:
        in_specs.append(_mod_spec(h_layer, modrow_of_tile, TM_LN))
        args.append(mods)
        out_specs.append(tile)
        out_shape.append(jax.ShapeDtypeStruct((T, D_MODEL), BF16))
    out = pl.pallas_call(
        functools.partial(_resid_ln_kernel, gate_row, h_row),
        grid=(T // TM_LN,),
        in_specs=in_specs,
        out_specs=out_specs,
        out_shape=out_shape,
        compiler_params=_cparams(("arbitrary",)),
        name="residual_ln",
    )(*args)
    return out if h_row is not None else (out[0], None)


def _fourier_kernel(u_ref, cc_hi_ref, cc_lo_ref, fme_hi_ref, fme_lo_ref, fmo_hi_ref, fmo_lo_ref, o_ref,
                    pe_ref, po_ref, rows_ref):
    L = u_ref.shape[1]
    H = L // 2
    rc = min(H, 256)
    for g in range(F_GROUPS):
        cols = slice(g * F_GROUP_W, (g + 1) * F_GROUP_W)
        rows_ref[...] = u_ref[0, :, cols]
        for parity, p_ref in enumerate((pe_ref, po_ref)):
            ug = rows_ref[pl.ds(parity, H, stride=2), :].astype(BF16)
            a = _bdot(ug, cc_hi_ref[...]) + _bdot(ug, cc_lo_ref[...])
            p_ref[0:H, cols] = a[:, :F_GROUP_W].astype(BF16)
            p_ref[H:L, cols] = a[:, F_GROUP_W:].astype(BF16)
    for r in range(0, H, rc):
        rows = slice(r, r + rc)
        e = _bdot(fme_hi_ref[rows, :], pe_ref[...]) + _bdot(fme_lo_ref[rows, :], pe_ref[...])
        o = _bdot(fmo_hi_ref[rows, :], po_ref[...]) + _bdot(fmo_lo_ref[rows, :], po_ref[...])
        o_ref[0, r:r + rc, :] = (e + o).astype(BF16)
        o_ref[0, H + r:H + r + rc, :] = (e - o).astype(BF16)


def _fourier_call(u3, L):
    B = u3.shape[0]
    H = L // 2
    mats = _dft_mats(L)
    cc_hi, cc_lo = _chan_dft()
    return pl.pallas_call(
        _fourier_kernel,
        grid=(B,),
        in_specs=[
            pl.BlockSpec((1, L, F_WIDTH), lambda b: (b, 0, 0)),
            _const_spec((F_GROUP_W, 2 * F_GROUP_W)),
            _const_spec((F_GROUP_W, 2 * F_GROUP_W)),
            _const_spec((H, L)), _const_spec((H, L)), _const_spec((H, L)), _const_spec((H, L)),
        ],
        out_specs=pl.BlockSpec((1, L, F_WIDTH), lambda b: (b, 0, 0)),
        out_shape=jax.ShapeDtypeStruct((B, L, F_WIDTH), BF16),
        scratch_shapes=[pltpu.VMEM((L, F_WIDTH), BF16), pltpu.VMEM((L, F_WIDTH), BF16),
                        pltpu.VMEM((L, F_GROUP_W), F32)],
        compiler_params=_cparams(("arbitrary",)),
        name="fourier_mix",
    )(u3, cc_hi, cc_lo, *mats["fme"], *mats["fmo"])


def _filter_taps_kernel(feat_ref, w1_ref, b1_ref, w2_ref, b2_ref, freq_ref, w3f_ref, w3b_ref, decay_ref,
                        fwd_ref, bsh_ref, h_ref):
    hp = lax.Precision.HIGHEST
    feats = feat_ref[...]

    @pl.when((pl.program_id(0) == 0) & (pl.program_id(1) == 0))
    def _():
        freq = freq_ref[...]
        h1 = jnp.sin(freq * (jnp.dot(feats, w1_ref[...], precision=hp, preferred_element_type=F32) + b1_ref[...]))
        h_ref[...] = jnp.sin(freq * (jnp.dot(h1, w2_ref[...], precision=hp, preferred_element_type=F32)
                                     + b2_ref[...]))

    h = h_ref[...]
    win = jnp.exp(-feats[:, 0:1] * decay_ref[...])
    fwd = jnp.dot(h, w3f_ref[...], precision=hp, preferred_element_type=F32) * win
    bwd = jnp.dot(h, w3b_ref[...], precision=hp, preferred_element_type=F32) * win
    row = lax.broadcasted_iota(jnp.int32, bwd.shape, 0)
    bsh = jnp.where(row == 0, 0.0, pltpu.roll(bwd, 1, axis=0))
    norm = jnp.sum(jnp.abs(fwd), axis=0, keepdims=True) + jnp.sum(jnp.abs(bsh), axis=0, keepdims=True)
    fwd_ref[...] = fwd / norm
    bsh_ref[...] = bsh / norm


def _filter_spectrum_kernel(fwd_ref, bsh_ref, f_hi_ref, f_lo_ref, gr_ref, gi_ref, gh_ref):
    L = fwd_ref.shape[0]
    fwd = fwd_ref[...]
    bsh = bsh_ref[...]
    both = jnp.concatenate([fwd, bsh], axis=1)
    b_hi = both.astype(BF16)
    b_lo = (both - b_hi.astype(F32)).astype(BF16)
    ct = fwd.shape[1]
    rc = min(L, 256)
    for r in range(0, L, rc):
        def spec(rows):
            return (_bdot(f_hi_ref[rows, :], b_hi) + _bdot(f_lo_ref[rows, :], b_hi)
                    + _bdot(f_hi_ref[rows, :], b_lo))
        c = spec(slice(r, r + rc))
        s = spec(slice(L + r, L + r + rc))
        gr_ref[r:r + rc, :] = c[:, :ct] + c[:, ct:]
        gi_ref[r:r + rc, :] = s[:, :ct] - s[:, ct:]
    q = lax.broadcasted_iota(jnp.int32, fwd.shape, 0) & 3
    c4 = jnp.where(q == 0, 1.0, jnp.where(q == 2, -1.0, 0.0))
    s4 = jnp.where(q == 1, 1.0, jnp.where(q == 3, -1.0, 0.0))
    gh_ref[0:1, :] = jnp.sum(c4 * (fwd + bsh), axis=0, keepdims=True) * (1.0 / L)
    gh_ref[1:2, :] = jnp.sum(s4 * (bsh - fwd), axis=0, keepdims=True) * (1.0 / L)


def _filters_call(L, w1, b1, w2, b2, w3, freq):
    t = np.arange(L, dtype=np.float32)
    t_norm = t / np.float32(max(L - 1, 1))
    w = np.float32(2.0 * math.pi) * t / np.float32(L)
    bands = np.linspace(1e-4, N_BANDS - 1, N_BANDS, dtype=np.float32)
    feats = np.zeros((L, LANE), np.float32)
    feats[:, 0] = t_norm
    feats[:, 1:1 + N_BANDS] = np.cos(w[:, None] * bands[None])
    feats[:, 1 + N_BANDS:POS_EMB] = np.sin(w[:, None] * bands[None])
    decay = np.abs(np.linspace(MIN_DECAY, MAX_DECAY, HY_WIDTH, dtype=np.float32))[None, :]
    w1p = jnp.zeros((LANE, FILTER_HID), F32).at[:POS_EMB].set(w1)
    nct = HY_WIDTH // HY_CT
    ocw = HY_ORDER * HY_WIDTH
    row = lambda v: v.reshape(1, -1)
    small = lambda shape: pl.BlockSpec(shape, lambda o, ct: (0, 0))
    fwd, bsh = pl.pallas_call(
        _filter_taps_kernel,
        grid=(HY_ORDER, nct),
        in_specs=[
            small((L, LANE)), small((LANE, FILTER_HID)), small((1, FILTER_HID)),
            small((FILTER_HID, FILTER_HID)), small((1, FILTER_HID)), small((1, FILTER_HID)),
            pl.BlockSpec((FILTER_HID, HY_CT), lambda o, ct: (0, o * 2 * nct + ct)),
            pl.BlockSpec((FILTER_HID, HY_CT), lambda o, ct: (0, o * 2 * nct + nct + ct)),
            pl.BlockSpec((1, HY_CT), lambda o, ct: (0, ct)),
        ],
        out_specs=[pl.BlockSpec((L, HY_CT), lambda o, ct: (0, o * nct + ct))] * 2,
        out_shape=[jax.ShapeDtypeStruct((L, ocw), F32)] * 2,
        scratch_shapes=[pltpu.VMEM((L, FILTER_HID), F32)],
        compiler_params=_cparams(("arbitrary", "arbitrary")),
        name="hyena_filter_taps",
    )(jnp.asarray(feats), w1p, row(b1), w2, row(b2), row(freq), w3, w3, jnp.asarray(decay))
    f_hi, f_lo = _dft_mats(L)["fsel"]
    return pl.pallas_call(
        _filter_spectrum_kernel,
        grid=(ocw // HY_CT,),
        in_specs=[
            pl.BlockSpec((L, HY_CT), lambda j: (0, j)),
            pl.BlockSpec((L, HY_CT), lambda j: (0, j)),
            _const_spec((2 * L, L)),
            _const_spec((2 * L, L)),
        ],
        out_specs=[pl.BlockSpec((L, HY_CT), lambda j: (0, j)),
                   pl.BlockSpec((L, HY_CT), lambda j: (0, j)),
                   pl.BlockSpec((2, HY_CT), lambda j: (0, j))],
        out_shape=[jax.ShapeDtypeStruct((L, ocw), F32), jax.ShapeDtypeStruct((L, ocw), F32),
                   jax.ShapeDtypeStruct((2, ocw), F32)],
        compiler_params=_cparams(("arbitrary",)),
        name="hyena_filter_spectrum",
    )(fwd, bsh, f_hi, f_lo)


def _hyena_kernel(v_ref, x1_ref, x2_ref, wv_ref, w1_ref, w2_ref, bv_ref, b1_ref, b2_ref,
                  gr0_ref, gi0_ref, gr1_ref, gi1_ref, gh0_ref, gh1_ref, skip_ref,
                  fe_hi_ref, fe_lo_ref, fo_hi_ref, fo_lo_ref, ie_hi_ref, ie_lo_ref, io_hi_ref, io_lo_ref,
                  o_ref, spec_e_ref, spec_o_ref, out_ref):
    L = v_ref.shape[1]
    H = L // 2
    rc = min(H, 256)
    row = lax.broadcasted_iota(jnp.int32, (H, HY_CT), 0)
    first = row == 0
    last = row == H - 1
    alt = jnp.where((row & 1) == 0, 1.0, -1.0)

    n_half = HY_CT // LANE

    def conv3(x_ref, w_ref, b_ref):
        for c in range(n_half):
            out_ref[c] = x_ref[0, :, c * LANE:(c + 1) * LANE]
        xe = jnp.concatenate([out_ref[c, pl.ds(0, H, stride=2), :] for c in range(n_half)], axis=1)
        xo = jnp.concatenate([out_ref[c, pl.ds(1, H, stride=2), :] for c in range(n_half)], axis=1)
        xo_prev = jnp.where(first, 0.0, pltpu.roll(xo, 1, axis=0))
        xe_next = jnp.where(last, 0.0, pltpu.roll(xe, H - 1, axis=0))
        w0, w1, w2, b = w_ref[0:1, :], w_ref[1:2, :], w_ref[2:3, :], b_ref[...]
        return xo_prev * w0 + xe * w1 + xo * w2 + b, xe * w0 + xo * w1 + xe_next * w2 + b

    ze, zo = conv3(v_ref, wv_ref, bv_ref)
    gates = (conv3(x1_ref, w1_ref, b1_ref), conv3(x2_ref, w2_ref, b2_ref))
    filt = ((gr0_ref, gi0_ref, gh0_ref), (gr1_ref, gi1_ref, gh1_ref))
    two_pass = lambda hi_ref, lo_ref, rows, x: _bdot(hi_ref[rows, :], x) + _bdot(lo_ref[rows, :], x)
    for o in range(HY_ORDER):
        gr_ref, gi_ref, gh_ref = filt[o]
        zeb = ze.astype(BF16)
        zob = zo.astype(BF16)
        for r in range(0, H, rc):
            re_rows, im_rows = slice(r, r + rc), slice(H + r, H + r + rc)
            er = two_pass(fe_hi_ref, fe_lo_ref, re_rows, zeb)
            ei = two_pass(fe_hi_ref, fe_lo_ref, im_rows, zeb)
            orr = two_pass(fo_hi_ref, fo_lo_ref, re_rows, zob)
            oi = two_pass(fo_hi_ref, fo_lo_ref, im_rows, zob)
            zr, zi, zpr, zpi = er + orr, ei + oi, er - orr, oi - ei
            gr, gi, gpr, gpi = gr_ref[re_rows, :], gi_ref[re_rows, :], gr_ref[im_rows, :], gi_ref[im_rows, :]
            yr, yi = zr * gr - zi * gi, zr * gi + zi * gr
            ypr, ypi = zpr * gpr - zpi * gpi, zpr * gpi + zpi * gpr
            spec_e_ref[re_rows, :] = (yr + ypr).astype(BF16)
            spec_e_ref[im_rows, :] = (yi - ypi).astype(BF16)
            spec_o_ref[re_rows, :] = (yr - ypr).astype(BF16)
            spec_o_ref[im_rows, :] = (yi + ypi).astype(BF16)
        zhr = jnp.sum(alt * ze, axis=0, keepdims=True)
        zhi = -jnp.sum(alt * zo, axis=0, keepdims=True)
        ghr, ghi = gh_ref[0:1, :], gh_ref[1:2, :]
        yhr, yhi = zhr * ghr - zhi * ghi, zhr * ghi + zhi * ghr
        ye, yo = [], []
        for r in range(0, H, rc):
            rows = slice(r, r + rc)
            ye.append(two_pass(ie_hi_ref, ie_lo_ref, rows, spec_e_ref[...]))
            yo.append(two_pass(io_hi_ref, io_lo_ref, rows, spec_o_ref[...]))
        ye = jnp.concatenate(ye, axis=0) + alt * yhr
        yo = jnp.concatenate(yo, axis=0) - alt * yhi
        skip = skip_ref[o:o + 1, :]
        ze = gates[o][0] * (ye + skip * ze)
        zo = gates[o][1] * (yo + skip * zo)
    for c in range(n_half):
        lanes = slice(c * LANE, (c + 1) * LANE)
        out_ref[c, pl.ds(0, H, stride=2), :] = ze[:, lanes]
        out_ref[c, pl.ds(1, H, stride=2), :] = zo[:, lanes]
        o_ref[0, :, lanes] = out_ref[c].astype(BF16)


def _hyena_call(u3, L, conv_w, conv_b, spectrum, skip):
    B = u3.shape[0]
    H = L // 2
    nct = HY_WIDTH // HY_CT
    hr, hi, hn = spectrum
    mats = _dft_mats(L)
    off = HY_OFF // HY_CT
    u_spec = lambda part: pl.BlockSpec((1, L, HY_CT), lambda b, ct: (b, 0, off + part * nct + ct))
    cw_spec = lambda part: pl.BlockSpec((3, HY_CT), lambda b, ct: (0, part * nct + ct))
    cb_spec = lambda part: pl.BlockSpec((1, HY_CT), lambda b, ct: (0, part * nct + ct))
    h_spec = lambda o: pl.BlockSpec((L, HY_CT), lambda b, ct: (0, o * nct + ct))
    n_spec = lambda o: pl.BlockSpec((2, HY_CT), lambda b, ct: (0, o * nct + ct))
    cb = conv_b.reshape(1, -1)
    return pl.pallas_call(
        _hyena_kernel,
        grid=(B, nct),
        in_specs=[
            u_spec(0), u_spec(1), u_spec(2),
            cw_spec(0), cw_spec(1), cw_spec(2),
            cb_spec(0), cb_spec(1), cb_spec(2),
            h_spec(0), h_spec(0), h_spec(1), h_spec(1), n_spec(0), n_spec(1),
            pl.BlockSpec((HY_ORDER, HY_CT), lambda b, ct: (0, ct)),
            _const_spec((L, H)), _const_spec((L, H)), _const_spec((L, H)), _const_spec((L, H)),
            _const_spec((H, L)), _const_spec((H, L)), _const_spec((H, L)), _const_spec((H, L)),
        ],
        out_specs=pl.BlockSpec((1, L, HY_CT), lambda b, ct: (b, 0, ct)),
        out_shape=jax.ShapeDtypeStruct((B, L, HY_WIDTH), BF16),
        scratch_shapes=[pltpu.VMEM((L, HY_CT), BF16), pltpu.VMEM((L, HY_CT), BF16),
                        pltpu.VMEM((HY_CT // LANE, L, LANE), F32)],
        compiler_params=_cparams(("arbitrary", "arbitrary")),
        name="hyena_mix",
    )(u3, u3, u3, conv_w, conv_w, conv_w, cb, cb, cb, hr, hi, hr, hi, hn, hn, skip,
      *mats["fe"], *mats["fo"], *mats["fe_t"], *mats["fo_t"])


def _sink_softmax_pv(scores, values, sink):
    m = sink
    for s in scores:
        m = jnp.maximum(m, jnp.max(s, axis=-1, keepdims=True))
    ps = [jnp.exp(s - m) for s in scores]
    den = jnp.exp(sink - m)
    for p in ps:
        den = den + jnp.sum(p, axis=-1, keepdims=True)
    inv = 1.0 / den
    out = None
    for p, v in zip(ps, values):
        t = _bdot((p * inv).astype(BF16), v)
        out = t if out is None else out + t
    return out


def _ctx_attn_kernel(sink_ref, q_ref, k_ref, v_ref, o_ref):
    h = pl.program_id(1)
    k = k_ref[0].astype(BF16)
    v = v_ref[0].astype(BF16)
    for g in range(GQA):
        cols = slice(g * HEAD_DIM, (g + 1) * HEAD_DIM)
        q = q_ref[0, :, cols].astype(BF16)
        s = lax.dot_general(q, k, (((1,), (1,)), ((), ())), preferred_element_type=F32) * SCALE
        o = _sink_softmax_pv([s], [v], sink_ref[h * GQA + g])
        o_ref[0, :, cols] = o.astype(BF16)


def _ctx_attn_call(u3, sink):
    B, S, _ = u3.shape
    gw = GQA * HEAD_DIM
    return pl.pallas_call(
        _ctx_attn_kernel,
        grid=(B, N_KV_HEADS),
        in_specs=[
            pl.BlockSpec(memory_space=pltpu.SMEM),
            pl.BlockSpec((1, S, gw), lambda b, h: (b, 0, Q_OFF // gw + h)),
            pl.BlockSpec((1, S, HEAD_DIM), lambda b, h: (b, 0, K_OFF // HEAD_DIM + h)),
            pl.BlockSpec((1, S, HEAD_DIM), lambda b, h: (b, 0, V_OFF // HEAD_DIM + h)),
        ],
        out_specs=pl.BlockSpec((1, S, gw), lambda b, h: (b, 0, h)),
        out_shape=jax.ShapeDtypeStruct((B, S, ATT_WIDTH), BF16),
        compiler_params=_cparams(("arbitrary", "arbitrary")),
        name="context_attention",
    )(sink, u3, u3, u3)


def _rope(x, cos, sin_signed):
    half = ROPE_AXIS_DIM // 2
    width = x.shape[1]
    lane = lax.broadcasted_iota(jnp.int32, x.shape, 1)
    low_half = (lane & (ROPE_AXIS_DIM - 1)) < half
    partner = jnp.where(low_half, pltpu.roll(x, width - half, axis=1), pltpu.roll(x, half, axis=1))
    return x * cos + partner * sin_signed


def _lat_attn_kernel(sink_ref, q_ref, k_ref, v_ref, ck_ref, cv_ref, cos_ref, sin_ref, o_ref, kr_ref, vb_ref):
    h = pl.program_id(1)
    L = k_ref.shape[1]
    nb = L // BLOCK
    rows4 = GQA * BLOCK
    c2 = SCALE * math.log2(math.e)
    kr_ref[...] = _rope(k_ref[0], cos_ref[:, :HEAD_DIM], sin_ref[:, :HEAD_DIM]).astype(BF16)
    vb_ref[...] = v_ref[0].astype(BF16)
    ck = ck_ref[0, 0].astype(BF16)
    cv = cv_ref[0, 0].astype(BF16)
    nt = (((1,), (1,)), ((), ()))

    r = lax.broadcasted_iota(jnp.int32, (rows4, 1), 0)
    sink = jnp.full((rows4, 1), sink_ref[h * GQA], F32)
    for g in range(1, GQA):
        sink = jnp.where(r >= g * BLOCK, sink_ref[h * GQA + g], sink)
    sink = sink * (1.0 / SCALE)
    a = r & (BLOCK - 1)
    b = lax.broadcasted_iota(jnp.int32, (1, BLOCK), 1)
    keep_prev = b >= a
    keep_next = b <= a

    for n in range(nb):
        rows = slice(n * BLOCK, (n + 1) * BLOCK)
        qa = _rope(q_ref[0, rows, :], cos_ref[rows, :], sin_ref[rows, :]).astype(BF16)
        q = jnp.concatenate([qa[:, g * HEAD_DIM:(g + 1) * HEAD_DIM] for g in range(GQA)], axis=0)
        scores, values = [], []
        for kb, keep in ((n - 1, keep_prev), (n, None), (n + 1, keep_next)):
            if 0 <= kb < nb:
                krows = slice(kb * BLOCK, (kb + 1) * BLOCK)
                s = lax.dot_general(q, kr_ref[krows, :], nt, preferred_element_type=F32)
                scores.append(s if keep is None else jnp.where(keep, s, -1e30))
                values.append(vb_ref[krows, :])
        scores.append(lax.dot_general(q, ck, nt, preferred_element_type=F32))
        values.append(cv)
        m = sink
        for s in scores:
            m = jnp.maximum(m, jnp.max(s, axis=-1, keepdims=True))
        ps = [jnp.exp2((s - m) * c2) for s in scores]
        den = jnp.exp2((sink - m) * c2)
        for p in ps:
            den = den + jnp.sum(p, axis=-1, keepdims=True)
        inv = 1.0 / den
        o = None
        for p, v in zip(ps, values):
            t = _bdot((p * inv).astype(BF16), v)
            o = t if o is None else o + t
        for g in range(GQA):
            o_ref[0, rows, g * HEAD_DIM:(g + 1) * HEAD_DIM] = o[g * BLOCK:(g + 1) * BLOCK].astype(BF16)


def _lat_attn_call(u3, cache_k_l, cache_v_l, layer, sink, cos, sin_signed):
    B, L, _ = u3.shape
    P = cache_k_l.shape[2]
    gw = GQA * HEAD_DIM
    c_spec = pl.BlockSpec((1, 1, P, HEAD_DIM), lambda b, h: (b, layer, 0, h))
    return pl.pallas_call(
        _lat_attn_kernel,
        grid=(B, N_KV_HEADS),
        in_specs=[
            pl.BlockSpec(memory_space=pltpu.SMEM),
            pl.BlockSpec((1, L, gw), lambda b, h: (b, 0, Q_OFF // gw + h)),
            pl.BlockSpec((1, L, HEAD_DIM), lambda b, h: (b, 0, K_OFF // HEAD_DIM + h)),
            pl.BlockSpec((1, L, HEAD_DIM), lambda b, h: (b, 0, V_OFF // HEAD_DIM + h)),
            c_spec, c_spec,
            _const_spec((L, gw)), _const_spec((L, gw)),
        ],
        out_specs=pl.BlockSpec((1, L, gw), lambda b, h: (b, 0, h)),
        out_shape=jax.ShapeDtypeStruct((B, L, ATT_WIDTH), BF16),
        scratch_shapes=[pltpu.VMEM((L, HEAD_DIM), BF16), pltpu.VMEM((L, HEAD_DIM), BF16)],
        compiler_params=_cparams(("arbitrary", "arbitrary")),
        name="latent_attention",
    )(sink, u3, u3, u3, cache_k_l, cache_v_l, jnp.tile(cos, (1, GQA)), jnp.tile(sin_signed, (1, GQA)))


def _rope_tables(L):
    rows = L // GRID_W
    row_pos = jnp.repeat(jnp.arange(rows), GRID_W)
    col_pos = jnp.arange(L) % GRID_W
    half = ROPE_AXIS_DIM // 2
    inv = ROPE_THETA ** (-jnp.arange(half, dtype=F32) * 2.0 / ROPE_AXIS_DIM)
    cos_parts, sin_parts = [], []
    for pos in (row_pos, col_pos):
        ang = pos.astype(F32)[:, None] * inv[None, :]
        cos_parts += [jnp.cos(ang), jnp.cos(ang)]
        sin_parts += [-jnp.sin(ang), jnp.sin(ang)]
    return jnp.concatenate(cos_parts, axis=-1), jnp.concatenate(sin_parts, axis=-1)


def _outproj_kernel(f_ref, hy_ref, att_ref, wf_ref, wh_ref, wa_ref, o_ref):
    o_ref[...] = (_bdot(f_ref[...], wf_ref[...]) + _bdot(hy_ref[...], wh_ref[...])
                  + _bdot(att_ref[...], wa_ref[...]))


def _outproj_call(f, hy, att, w_out, layer):
    T = f.shape[0]
    assert F_WIDTH == HY_WIDTH and ATT_WIDTH == F_WIDTH + HY_WIDTH
    return pl.pallas_call(
        _outproj_kernel,
        grid=(T // TM_MM, D_MODEL // TN_MM),
        in_specs=[
            pl.BlockSpec((TM_MM, F_WIDTH), lambda i, j: (i, 0)),
            pl.BlockSpec((TM_MM, HY_WIDTH), lambda i, j: (i, 0)),
            pl.BlockSpec((TM_MM, ATT_WIDTH), lambda i, j: (i, 0)),
            pl.BlockSpec((None, F_WIDTH, TN_MM), lambda i, j: (layer, 0, j)),
            pl.BlockSpec((None, HY_WIDTH, TN_MM), lambda i, j: (layer, 1, j)),
            pl.BlockSpec((None, ATT_WIDTH, TN_MM), lambda i, j: (layer, 1, j)),
        ],
        out_specs=pl.BlockSpec((TM_MM, TN_MM), lambda i, j: (i, j)),
        out_shape=jax.ShapeDtypeStruct((T, D_MODEL), F32),
        compiler_params=_cparams(("arbitrary", "arbitrary")),
        name="outproj",
    )(f, hy, att, w_out, w_out, w_out)


def _ffn_kernel(seq_len, nf, h_ref, wg_ref, wv_ref, cwg_ref, cwv_ref, cbg_ref, cbv_ref, wd_ref, o_ref,
                u_ref, act_ref):
    g = pl.program_id(0)
    tm = h_ref.shape[0]
    seg = min(seq_len, tm)
    n_seg = tm // seg
    base = [SUBLANE + s * (seg + SUBLANE) for s in range(n_seg)]

    @pl.when(g == 0)
    def _():
        u_ref[...] = jnp.zeros(u_ref.shape, F32)
        act_ref[...] = jnp.zeros(act_ref.shape, BF16)

    @pl.when((g == 0) | (lax.rem(g + (nf - 2), nf) == 0))
    def _():
        o_ref[...] = jnp.zeros(o_ref.shape, F32)

    live = (g > 0).astype(F32)

    def conv_piece(slot, p):
        rp = tm // FFN_CONV_PIECES
        s, r0 = divmod(p * rp, seg)
        lo = base[s] + r0

        def conv(part, cw_ref, cb_ref):
            win = lambda off: u_ref[1 - slot, part, lo + off:lo + off + rp, :]
            return win(-1) * cw_ref[0:1, :] + win(0) * cw_ref[1:2, :] + win(1) * cw_ref[2:3, :] + cb_ref[...]

        gate = conv(0, cwg_ref, cbg_ref)
        val = conv(1, cwv_ref, cbv_ref)
        act_ref[1 - slot, p * rp:(p + 1) * rp, :] = (_silu(gate) * (val * live)).astype(BF16)

    def down_cols(slot, p):
        cp = D_MODEL // FFN_PIECES
        cols = slice(p * cp, (p + 1) * cp)
        o_ref[:, cols] += _bdot(act_ref[slot], wd_ref[:, cols].astype(BF16))

    def up_rows(slot, part, r):
        w_ref = (wg_ref, wv_ref)[part]
        s, r0 = divmod(r * up_rb, seg)
        u_ref[slot, part, base[s] + r0:base[s] + r0 + up_rb, :] = _bdot(
            h_ref[r * up_rb:(r + 1) * up_rb, :], w_ref[...])

    up_rb = min(seg, FFN_UP_ROWS)
    segments = [(functools.partial(down_cols, p=p), tm * (D_MODEL // FFN_PIECES)) for p in range(FFN_PIECES)]
    segments += [(functools.partial(up_rows, part=part, r=r), up_rb * D_MODEL)
                 for part in (0, 1) for r in range(tm // up_rb)]
    total_time = sum(t for _, t in segments[:-1])

    def step(slot):
        done, emitted = 0, 0
        for run, t in segments:
            while emitted < FFN_CONV_PIECES and emitted * total_time <= done * FFN_CONV_PIECES:
                conv_piece(slot, emitted)
                emitted += 1
            run(slot)
            done += t
        for p in range(emitted, FFN_CONV_PIECES):
            conv_piece(slot, p)

    for slot in (0, 1):
        pl.when((g & 1) == slot)(functools.partial(step, slot))


def _ffn_call(h, seq_len, w_up, conv_w, conv_b, w_down, layer):
    T = h.shape[0]
    nf = D_FF // TF_FFN
    n_tiles = T // TM_FFN
    assert TM_FFN % seq_len == 0
    cb = conv_b.reshape(DEPTH, 1, -1)
    wspec = lambda shape, imap: pl.BlockSpec((None,) + shape, imap)
    seg = min(seq_len, TM_FFN)
    u_rows = SUBLANE + (TM_FFN // seg) * (seg + SUBLANE)
    up_chunk = lambda g: lax.rem(g, nf)
    cv_chunk = lambda g: lax.rem(g + (nf - 1), nf)
    dn_chunk = lambda g: lax.rem(g + (nf - 2), nf)
    up_tile = lambda g: jnp.minimum(g // nf, n_tiles - 1)
    dn_tile = lambda g: jnp.clip((g + (nf - 2)) // nf - 1, 0, n_tiles - 1)
    return pl.pallas_call(
        functools.partial(_ffn_kernel, seq_len, nf),
        grid=(n_tiles * nf + 2,),
        in_specs=[
            pl.BlockSpec((TM_FFN, D_MODEL), lambda g: (up_tile(g), 0), pipeline_mode=pl.Buffered(1)),
            wspec((D_MODEL, TF_FFN), lambda g: (layer, 0, up_chunk(g))),
            wspec((D_MODEL, TF_FFN), lambda g: (layer, 0, nf + up_chunk(g))),
            wspec((3, TF_FFN), lambda g: (layer, 0, cv_chunk(g))),
            wspec((3, TF_FFN), lambda g: (layer, 0, nf + cv_chunk(g))),
            wspec((1, TF_FFN), lambda g: (layer, 0, cv_chunk(g))),
            wspec((1, TF_FFN), lambda g: (layer, 0, nf + cv_chunk(g))),
            wspec((TF_FFN, D_MODEL), lambda g: (layer, dn_chunk(g), 0)),
        ],
        out_specs=pl.BlockSpec((TM_FFN, D_MODEL), lambda g: (dn_tile(g), 0), pipeline_mode=pl.Buffered(1)),
        out_shape=jax.ShapeDtypeStruct((T, D_MODEL), F32),
        scratch_shapes=[pltpu.VMEM((2, 2, u_rows, TF_FFN), F32), pltpu.VMEM((2, TM_FFN, TF_FFN), BF16)],
        compiler_params=_cparams(("arbitrary",)),
        name="conv_ffn",
    )(h, w_up, w_up, conv_w, conv_w, cb, cb, w_down)


def _layer(x, h, B, L, layer, mods, modrow_of_tile, big, p, attend):
    T = B * L
    u3 = _inproj_call(h, big["w_in"], layer).reshape(B, L, IN_WIDTH)
    f = _fourier_call(u3, L)
    spectrum = _filters_call(L, p["flt_w1"], p["flt_b1"], p["flt_w2"], p["flt_b2"], p["flt_w3"], p["flt_freq"])
    hy = _hyena_call(u3, L, p["hy_conv_w"], p["hy_conv_b"], spectrum, p["hy_skip"])
    att = attend(u3)
    proj = _outproj_call(f.reshape(T, F_WIDTH), hy.reshape(T, HY_WIDTH), att.reshape(T, ATT_WIDTH),
                         big["w_out"], layer)
    x, h2 = _resid_ln_call(x, proj, mods, layer, 2, big["ln1_g"], big["ln1_b"], modrow_of_tile,
                           h_layer=layer, h_row=3)
    ff = _ffn_call(h2, L, big["ffn_w_up"], big["ffn_conv_w"], big["ffn_conv_b"], big["ffn_w_down"], layer)
    if layer + 1 < DEPTH:
        x, h_next = _resid_ln_call(x, ff, mods, layer, 5, big["ln2_g"], big["ln2_b"], modrow_of_tile,
                                   h_layer=layer + 1, h_row=0)
    else:
        x, h_next = _resid_ln_call(x, ff, mods, layer, 5, big["ln2_g"], big["ln2_b"], modrow_of_tile)
    return x, h_next, u3


def kernel(x_prompt, x_sample, cache_k, cache_v, c, c_ctx, w_mod, b_mod, w_in, w_out, attn_sink,
           hy_conv_w, hy_conv_b, flt_w1, flt_b1, flt_w2, flt_b2, flt_w3, flt_freq, hy_skip,
           ln1_g, ln1_b, ffn_w_up, ffn_conv_w, ffn_conv_b, ffn_w_down, ln2_g, ln2_b):
    Bp, S, _ = x_prompt.shape
    Bs, Ls, _ = x_sample.shape
    past = cache_k.shape[2]

    cvec = jnp.zeros((MOD_ROWS, D_MODEL), F32).at[0].set(c_ctx).at[1:1 + Bs].set(c)
    mods = _mod_call(cvec, w_mod, b_mod).reshape(DEPTH, MOD_ROWS, 6, D_MODEL)

    big = dict(w_in=w_in.astype(BF16), w_out=w_out.astype(BF16), ffn_w_up=ffn_w_up.astype(BF16),
               ffn_w_down=ffn_w_down, ffn_conv_w=ffn_conv_w, ffn_conv_b=ffn_conv_b,
               ln1_g=ln1_g, ln1_b=ln1_b, ln2_g=ln2_g, ln2_b=ln2_b)
    ck = cache_k.reshape(Bs, DEPTH, past, KV_WIDTH)
    cv = cache_v.reshape(Bs, DEPTH, past, KV_WIDTH)
    cos, sin_signed = _rope_tables(Ls)

    ctx_row = lambda i, tm: 0
    lat_row = lambda i, tm: 1 + i * tm // Ls

    xp = x_prompt.reshape(Bp * S, D_MODEL)
    xs = x_sample.reshape(Bs * Ls, D_MODEL)
    hp = _ln_mod_call(xp, mods, 0, ctx_row)
    hs = _ln_mod_call(xs, mods, 0, lat_row)
    ctx_k, ctx_v = [], []
    for l in range(DEPTH):
        p = dict(flt_w1=flt_w1[l], flt_b1=flt_b1[l], flt_w2=flt_w2[l], flt_b2=flt_b2[l], flt_w3=flt_w3[l],
                 flt_freq=flt_freq[l], hy_conv_w=hy_conv_w[l], hy_conv_b=hy_conv_b[l], hy_skip=hy_skip[l])
        sink = attn_sink[l]
        xp, hp, up3 = _layer(xp, hp, Bp, S, l, mods, ctx_row, big, p, lambda u3: _ctx_attn_call(u3, sink))
        ctx_k.append(up3[:, :, K_OFF:K_OFF + KV_WIDTH].reshape(Bp, S, N_KV_HEADS, HEAD_DIM))
        ctx_v.append(up3[:, :, V_OFF:V_OFF + KV_WIDTH].reshape(Bp, S, N_KV_HEADS, HEAD_DIM))
        xs, hs, _ = _layer(xs, hs, Bs, Ls, l, mods, lat_row, big, p,
                           lambda u3: _lat_attn_call(u3, ck, cv, l, sink, cos, sin_signed))

    return (xp.reshape(Bp, S, D_MODEL), xs.reshape(Bs, Ls, D_MODEL),
            jnp.stack(ctx_k, axis=1), jnp.stack(ctx_v, axis=1))
```

```python
import functools
import math

import ml_dtypes
import numpy as np
import jax
import jax.numpy as jnp
from jax import lax
from jax.experimental import pallas as pl
from jax.experimental.pallas import tpu as pltpu

F32 = jnp.float32
BF16 = jnp.bfloat16

D_MODEL = 4096
DEPTH = 2
GRID_W = 64
F_WIDTH = D_MODEL // 4
F_GROUPS = 8
F_GROUP_W = F_WIDTH // F_GROUPS
HY_WIDTH = D_MODEL // 4
HY_ORDER = 2
N_BANDS = 16
POS_EMB = 1 + 2 * N_BANDS
FILTER_HID = 64
DECAY_TARGET = 1e-2
MIN_DECAY = math.log(DECAY_TARGET) / 1.5
MAX_DECAY = math.log(DECAY_TARGET) / 0.3
HEAD_DIM = 128
N_Q_HEADS = (D_MODEL // 2) // HEAD_DIM
N_KV_HEADS = N_Q_HEADS // 4
GQA = N_Q_HEADS // N_KV_HEADS
ATT_WIDTH = N_Q_HEADS * HEAD_DIM
KV_WIDTH = N_KV_HEADS * HEAD_DIM
WINDOW = 128
BLOCK = 128
SCALE = HEAD_DIM ** -0.5
ROPE_THETA = 10000.0
ROPE_AXIS_DIM = HEAD_DIM // 2
D_FF = 256 * ((8 * D_MODEL // 3 + 255) // 256)
LN_EPS = 1e-5
ALPHA = (2 * DEPTH) ** 0.25
IN_WIDTH = F_WIDTH + 3 * HY_WIDTH + ATT_WIDTH + 2 * KV_WIDTH
MIX_WIDTH = F_WIDTH + HY_WIDTH + ATT_WIDTH

HY_OFF = F_WIDTH
Q_OFF = F_WIDTH + 3 * HY_WIDTH
K_OFF = Q_OFF + ATT_WIDTH
V_OFF = K_OFF + KV_WIDTH

MOD_ROWS = 16
LANE = 128
SUBLANE = 8
VMEM_LIMIT = 56 * 1024 * 1024

BF16_ROWS = 16
TM_MM = 1024
TN_MM = 512
TM_LN = 256
TM_FFN = 1024
TF_FFN = 256
FFN_UP_ROWS = 256
FFN_PIECES = 16
FFN_DOWN_SHARE = 2
FFN_CONV_PIECES = 16
HY_CT = 256
TN_MOD = 512
TK_MOD = 512
ROW_CHUNK = 32


def _cparams(sem):
    return pltpu.CompilerParams(dimension_semantics=sem, vmem_limit_bytes=VMEM_LIMIT)


def _const_spec(shape):
    return pl.BlockSpec(shape, lambda *_: (0,) * len(shape), pipeline_mode=pl.Buffered(1))


def _bdot(a, b):
    return jnp.dot(a, b, preferred_element_type=F32)


def _ln_rows(x):
    mu = jnp.mean(x, axis=-1, keepdims=True)
    xc = x - mu
    var = jnp.mean(xc * xc, axis=-1, keepdims=True)
    return xc * lax.rsqrt(var + LN_EPS)


def _silu(x):
    return x / (1.0 + jnp.exp(-x))


def _for_row_chunks(n_rows, body):
    def step(c, carry):
        body(pl.ds(pl.multiple_of(c * ROW_CHUNK, ROW_CHUNK), ROW_CHUNK))
        return carry
    lax.fori_loop(0, n_rows // ROW_CHUNK, step, 0)


def _hi_lo(x):
    hi = x.astype(ml_dtypes.bfloat16)
    lo = (x - hi.astype(np.float64)).astype(ml_dtypes.bfloat16)
    return jnp.asarray(hi), jnp.asarray(lo)


@functools.lru_cache(maxsize=None)
def _dft_mats(L):
    H = L // 2
    k = np.arange(H)
    t = np.arange(H)
    ang_e = 2.0 * np.pi * ((k[:, None] * t[None, :]) % L) / L
    fe = np.concatenate([np.cos(ang_e), -np.sin(ang_e)], axis=0)
    ang_o = np.pi * ((k[:, None] * (2 * t[None, :] + 1)) % (2 * L)) / L
    fo = np.concatenate([np.cos(ang_o), -np.sin(ang_o)], axis=0)
    bins = np.concatenate([k, L - k])
    n = np.arange(L)
    ang_s = np.pi * ((bins[:, None] * n[None, :]) % (2 * L)) / L
    wgt = np.where(np.concatenate([k, k]) == 0, 1.0, 2.0)[:, None] / (2 * L)
    fsel = np.concatenate([np.cos(ang_s) * wgt, -np.sin(ang_s) * wgt], axis=0)
    ang_me = 2.0 * np.pi * ((k[:, None] * (2 * t[None, :])) % L) / L
    ang_mo = 2.0 * np.pi * ((k[:, None] * (2 * t[None, :] + 1)) % L) / L
    fme = np.concatenate([np.cos(ang_me), -np.sin(ang_me)], axis=1) / np.sqrt(L)
    fmo = np.concatenate([np.cos(ang_mo), -np.sin(ang_mo)], axis=1) / np.sqrt(L)
    return dict(fe=_hi_lo(fe), fo=_hi_lo(fo), fe_t=_hi_lo(fe.T), fo_t=_hi_lo(fo.T), fsel=_hi_lo(fsel),
                fme=_hi_lo(fme), fmo=_hi_lo(fmo))


@functools.lru_cache(maxsize=None)
def _chan_dft():
    n = np.arange(F_GROUP_W)
    ang = 2.0 * np.pi * ((n[:, None] * n[None, :]) % F_GROUP_W) / F_GROUP_W
    return _hi_lo(np.concatenate([np.cos(ang), np.sin(ang)], axis=1) / np.sqrt(F_GROUP_W))


def _mod_kernel(c_ref, w_ref, b_ref, o_ref):
    acc = b_ref[0] + jnp.zeros((MOD_ROWS, TN_MOD), F32)
    for k in range(0, D_MODEL, TK_MOD):
        s = _silu(c_ref[:, k:k + TK_MOD]).astype(BF16)
        acc = acc + _bdot(s, w_ref[0, k:k + TK_MOD, :].astype(BF16))
    o_ref[0] = acc


def _mod_call(cvec, w_mod, b_mod):
    n_out = w_mod.shape[-1]
    return pl.pallas_call(
        _mod_kernel,
        grid=(DEPTH, n_out // TN_MOD),
        in_specs=[
            pl.BlockSpec((MOD_ROWS, D_MODEL), lambda l, j: (0, 0)),
            pl.BlockSpec((1, D_MODEL, TN_MOD), lambda l, j: (l, 0, j)),
            pl.BlockSpec((1, 1, TN_MOD), lambda l, j: (l, 0, j)),
        ],
        out_specs=pl.BlockSpec((1, MOD_ROWS, TN_MOD), lambda l, j: (l, 0, j)),
        out_shape=jax.ShapeDtypeStruct((DEPTH, MOD_ROWS, n_out), F32),
        compiler_params=_cparams(("arbitrary", "arbitrary")),
        name="adaln_mod",
    )(cvec, w_mod, b_mod.reshape(DEPTH, 1, n_out))


def _inproj_kernel(h_ref, w_ref, o_ref):
    o_ref[...] = _bdot(h_ref[...], w_ref[...])


def _inproj_call(h, w_in, layer):
    T = h.shape[0]
    return pl.pallas_call(
        _inproj_kernel,
        grid=(T // TM_MM, IN_WIDTH // TN_MM),
        in_specs=[
            pl.BlockSpec((TM_MM, D_MODEL), lambda i, j: (i, 0)),
            pl.BlockSpec((None, D_MODEL, TN_MM), lambda i, j: (layer, 0, j)),
        ],
        out_specs=pl.BlockSpec((TM_MM, TN_MM), lambda i, j: (i, j)),
        out_shape=jax.ShapeDtypeStruct((T, IN_WIDTH), F32),
        compiler_params=_cparams(("arbitrary", "arbitrary")),
        name="inproj",
    )(h, w_in)


def _mod_spec(layer, modrow_of_tile, tm):
    return pl.BlockSpec((None, None, 6, D_MODEL), lambda i: (layer, modrow_of_tile(i, tm), 0, 0))


def _ln_mod_kernel(x_ref, mod_ref, h_ref):
    def body(rows):
        h = _ln_rows(x_ref[rows, :]) * (1.0 + mod_ref[1:2, :]) + mod_ref[0:1, :]
        h_ref[rows, :] = h.astype(BF16)
    _for_row_chunks(TM_LN, body)


def _ln_mod_call(x, mods, layer, modrow_of_tile):
    T = x.shape[0]
    return pl.pallas_call(
        _ln_mod_kernel,
        grid=(T // TM_LN,),
        in_specs=[pl.BlockSpec((TM_LN, D_MODEL), lambda i: (i, 0)), _mod_spec(layer, modrow_of_tile, TM_LN)],
        out_specs=pl.BlockSpec((TM_LN, D_MODEL), lambda i: (i, 0)),
        out_shape=jax.ShapeDtypeStruct((T, D_MODEL), BF16),
        compiler_params=_cparams(("arbitrary",)),
        name="ln_modulate",
    )(x, mods)


def _resid_ln_kernel(gate_row, h_row, x_ref, d_ref, modg_ref, g_ref, b_ref, *rest):
    if h_row is None:
        (xo_ref,) = rest
    else:
        modh_ref, xo_ref, ho_ref = rest

    def body(rows):
        y = ALPHA * x_ref[rows, :] + modg_ref[gate_row:gate_row + 1, :] * d_ref[rows, :]
        xn = _ln_rows(y) * g_ref[...] + b_ref[...]
        xo_ref[rows, :] = xn
        if h_row is not None:
            h = _ln_rows(xn) * (1.0 + modh_ref[h_row + 1:h_row + 2, :]) + modh_ref[h_row:h_row + 1, :]
            ho_ref[rows, :] = h.astype(BF16)
    _for_row_chunks(TM_LN, body)


def _resid_ln_call(x, d, mods, layer, gate_row, ln_g, ln_b, modrow_of_tile, h_layer=None, h_row=None):
    T = x.shape[0]
    tile = pl.BlockSpec((TM_LN, D_MODEL), lambda i: (i, 0))
    vec = pl.BlockSpec((None, 1, D_MODEL), lambda i: (layer, 0, 0))
    in_specs = [tile, tile, _mod_spec(layer, modrow_of_tile, TM_LN), vec, vec]
    args = [x, d, mods, ln_g.reshape(DEPTH, 1, D_MODEL), ln_b.reshape(DEPTH, 1, D_MODEL)]
    out_specs = [tile]
    out_shape = [jax.ShapeDtypeStruct((T, D_MODEL), F32)]
    if h_row is not None:
        in_specs.append(_mod_spec(h_layer, modrow_of_tile, TM_LN))
        args.append(mods)
        out_specs.append(tile)
        out_shape.append(jax.ShapeDtypeStruct((T, D_MODEL), BF16))
    out = pl.pallas_call(
        functools.partial(_resid_ln_kernel, gate_row, h_row),
        grid=(T // TM_LN,),
        in_specs=in_specs,
        out_specs=out_specs,
        out_shape=out_shape,
        compiler_params=_cparams(("arbitrary",)),
        name="residual_ln",
    )(*args)
    return out if h_row is not None else (out[0], None)


def _fourier_kernel(u_ref, cc_hi_ref, cc_lo_ref, fme_hi_ref, fme_lo_ref, fmo_hi_ref, fmo_lo_ref, o_ref,
                    pe_ref, po_ref, rows_ref):
    L = u_ref.shape[1]
    H = L // 2
    rc = min(H, 256)
    for g in range(F_GROUPS):
        cols = slice(g * F_GROUP_W, (g + 1) * F_GROUP_W)
        rows_ref[...] = u_ref[0, :, cols]
        for parity, p_ref in enumerate((pe_ref, po_ref)):
            ug = rows_ref[pl.ds(parity, H, stride=2), :].astype(BF16)
            a = _bdot(ug, cc_hi_ref[...]) + _bdot(ug, cc_lo_ref[...])
            p_ref[0:H, cols] = a[:, :F_GROUP_W].astype(BF16)
            p_ref[H:L, cols] = a[:, F_GROUP_W:].astype(BF16)
    for r in range(0, H, rc):
        rows = slice(r, r + rc)
        e = _bdot(fme_hi_ref[rows, :], pe_ref[...]) + _bdot(fme_lo_ref[rows, :], pe_ref[...])
        o = _bdot(fmo_hi_ref[rows, :], po_ref[...]) + _bdot(fmo_lo_ref[rows, :], po_ref[...])
        o_ref[0, r:r + rc, :] = (e + o).astype(BF16)
        o_ref[0, H + r:H + r + rc, :] = (e - o).astype(BF16)


def _fourier_call(u3, L):
    B = u3.shape[0]
    H = L // 2
    mats = _dft_mats(L)
    cc_hi, cc_lo = _chan_dft()
    return pl.pallas_call(
        _fourier_kernel,
        grid=(B,),
        in_specs=[
            pl.BlockSpec((1, L, F_WIDTH), lambda b: (b, 0, 0)),
            _const_spec((F_GROUP_W, 2 * F_GROUP_W)),
            _const_spec((F_GROUP_W, 2 * F_GROUP_W)),
            _const_spec((H, L)), _const_spec((H, L)), _const_spec((H, L)), _const_spec((H, L)),
        ],
        out_specs=pl.BlockSpec((1, L, F_WIDTH), lambda b: (b, 0, 0)),
        out_shape=jax.ShapeDtypeStruct((B, L, F_WIDTH), BF16),
        scratch_shapes=[pltpu.VMEM((L, F_WIDTH), BF16), pltpu.VMEM((L, F_WIDTH), BF16),
                        pltpu.VMEM((L, F_GROUP_W), F32)],
        compiler_params=_cparams(("arbitrary",)),
        name="fourier_mix",
    )(u3, cc_hi, cc_lo, *mats["fme"], *mats["fmo"])


def _filter_taps_kernel(feat_ref, w1_ref, b1_ref, w2_ref, b2_ref, freq_ref, w3f_ref, w3b_ref, decay_ref,
                        fwd_ref, bsh_ref, h_ref):
    hp = lax.Precision.HIGHEST
    feats = feat_ref[...]

    @pl.when((pl.program_id(0) == 0) & (pl.program_id(1) == 0))
    def _():
        freq = freq_ref[...]
        h1 = jnp.sin(freq * (jnp.dot(feats, w1_ref[...], precision=hp, preferred_element_type=F32) + b1_ref[...]))
        h_ref[...] = jnp.sin(freq * (jnp.dot(h1, w2_ref[...], precision=hp, preferred_element_type=F32)
                                     + b2_ref[...]))

    h = h_ref[...]
    win = jnp.exp(-feats[:, 0:1] * decay_ref[...])
    fwd = jnp.dot(h, w3f_ref[...], precision=hp, preferred_element_type=F32) * win
    bwd = jnp.dot(h, w3b_ref[...], precision=hp, preferred_element_type=F32) * win
    row = lax.broadcasted_iota(jnp.int32, bwd.shape, 0)
    bsh = jnp.where(row == 0, 0.0, pltpu.roll(bwd, 1, axis=0))
    norm = jnp.sum(jnp.abs(fwd), axis=0, keepdims=True) + jnp.sum(jnp.abs(bsh), axis=0, keepdims=True)
    fwd_ref[...] = fwd / norm
    bsh_ref[...] = bsh / norm


def _filter_spectrum_kernel(fwd_ref, bsh_ref, f_hi_ref, f_lo_ref, gr_ref, gi_ref, gh_ref):
    L = fwd_ref.shape[0]
    fwd = fwd_ref[...]
    bsh = bsh_ref[...]
    both = jnp.concatenate([fwd, bsh], axis=1)
    b_hi = both.astype(BF16)
    b_lo = (both - b_hi.astype(F32)).astype(BF16)
    ct = fwd.shape[1]
    rc = min(L, 256)
    for r in range(0, L, rc):
        def spec(rows):
            return (_bdot(f_hi_ref[rows, :], b_hi) + _bdot(f_lo_ref[rows, :], b_hi)
                    + _bdot(f_hi_ref[rows, :], b_lo))
        c = spec(slice(r, r + rc))
        s = spec(slice(L + r, L + r + rc))
        gr_ref[r:r + rc, :] = c[:, :ct] + c[:, ct:]
        gi_ref[r:r + rc, :] = s[:, :ct] - s[:, ct:]
    q = lax.broadcasted_iota(jnp.int32, fwd.shape, 0) & 3
    c4 = jnp.where(q == 0, 1.0, jnp.where(q == 2, -1.0, 0.0))
    s4 = jnp.where(q == 1, 1.0, jnp.where(q == 3, -1.0, 0.0))
    gh_ref[0:1, :] = jnp.sum(c4 * (fwd + bsh), axis=0, keepdims=True) * (1.0 / L)
    gh_ref[1:2, :] = jnp.sum(s4 * (bsh - fwd), axis=0, keepdims=True) * (1.0 / L)


def _filters_call(L, w1, b1, w2, b2, w3, freq):
    t = np.arange(L, dtype=np.float32)
    t_norm = t / np.float32(max(L - 1, 1))
    w = np.float32(2.0 * math.pi) * t / np.float32(L)
    bands = np.linspace(1e-4, N_BANDS - 1, N_BANDS, dtype=np.float32)
    feats = np.zeros((L, LANE), np.float32)
    feats[:, 0] = t_norm
    feats[:, 1:1 + N_BANDS] = np.cos(w[:, None] * bands[None])
    feats[:, 1 + N_BANDS:POS_EMB] = np.sin(w[:, None] * bands[None])
    decay = np.abs(np.linspace(MIN_DECAY, MAX_DECAY, HY_WIDTH, dtype=np.float32))[None, :]
    w1p = jnp.zeros((LANE, FILTER_HID), F32).at[:POS_EMB].set(w1)
    nct = HY_WIDTH // HY_CT
    ocw = HY_ORDER * HY_WIDTH
    row = lambda v: v.reshape(1, -1)
    small = lambda shape: pl.BlockSpec(shape, lambda o, ct: (0, 0))
    fwd, bsh = pl.pallas_call(
        _filter_taps_kernel,
        grid=(HY_ORDER, nct),
        in_specs=[
            small((L, LANE)), small((LANE, FILTER_HID)), small((1, FILTER_HID)),
            small((FILTER_HID, FILTER_HID)), small((1, FILTER_HID)), small((1, FILTER_HID)),
            pl.BlockSpec((FILTER_HID, HY_CT), lambda o, ct: (0, o * 2 * nct + ct)),
            pl.BlockSpec((FILTER_HID, HY_CT), lambda o, ct: (0, o * 2 * nct + nct + ct)),
            pl.BlockSpec((1, HY_CT), lambda o, ct: (0, ct)),
        ],
        out_specs=[pl.BlockSpec((L, HY_CT), lambda o, ct: (0, o * nct + ct))] * 2,
        out_shape=[jax.ShapeDtypeStruct((L, ocw), F32)] * 2,
        scratch_shapes=[pltpu.VMEM((L, FILTER_HID), F32)],
        compiler_params=_cparams(("arbitrary", "arbitrary")),
        name="hyena_filter_taps",
    )(jnp.asarray(feats), w1p, row(b1), w2, row(b2), row(freq), w3, w3, jnp.asarray(decay))
    f_hi, f_lo = _dft_mats(L)["fsel"]
    return pl.pallas_call(
        _filter_spectrum_kernel,
        grid=(ocw // HY_CT,),
        in_specs=[
            pl.BlockSpec((L, HY_CT), lambda j: (0, j)),
            pl.BlockSpec((L, HY_CT), lambda j: (0, j)),
            _const_spec((2 * L, L)),
            _const_spec((2 * L, L)),
        ],
        out_specs=[pl.BlockSpec((L, HY_CT), lambda j: (0, j)),
                   pl.BlockSpec((L, HY_CT), lambda j: (0, j)),
                   pl.BlockSpec((2, HY_CT), lambda j: (0, j))],
        out_shape=[jax.ShapeDtypeStruct((L, ocw), F32), jax.ShapeDtypeStruct((L, ocw), F32),
                   jax.ShapeDtypeStruct((2, ocw), F32)],
        compiler_params=_cparams(("arbitrary",)),
        name="hyena_filter_spectrum",
    )(fwd, bsh, f_hi, f_lo)


def _hyena_kernel(v_ref, x1_ref, x2_ref, wv_ref, w1_ref, w2_ref, bv_ref, b1_ref, b2_ref,
                  gr0_ref, gi0_ref, gr1_ref, gi1_ref, gh0_ref, gh1_ref, skip_ref,
                  fe_hi_ref, fe_lo_ref, fo_hi_ref, fo_lo_ref, ie_hi_ref, ie_lo_ref, io_hi_ref, io_lo_ref,
                  o_ref, spec_e_ref, spec_o_ref, out_ref):
    L = v_ref.shape[1]
    H = L // 2
    rc = min(H, 256)
    row = lax.broadcasted_iota(jnp.int32, (H, HY_CT), 0)
    first = row == 0
    last = row == H - 1
    alt = jnp.where((row & 1) == 0, 1.0, -1.0)

    n_half = HY_CT // LANE

    def conv3(x_ref, w_ref, b_ref):
        for c in range(n_half):
            out_ref[c] = x_ref[0, :, c * LANE:(c + 1) * LANE]
        xe = jnp.concatenate([out_ref[c, pl.ds(0, H, stride=2), :] for c in range(n_half)], axis=1)
        xo = jnp.concatenate([out_ref[c, pl.ds(1, H, stride=2), :] for c in range(n_half)], axis=1)
        xo_prev = jnp.where(first, 0.0, pltpu.roll(xo, 1, axis=0))
        xe_next = jnp.where(last, 0.0, pltpu.roll(xe, H - 1, axis=0))
        w0, w1, w2, b = w_ref[0:1, :], w_ref[1:2, :], w_ref[2:3, :], b_ref[...]
        return xo_prev * w0 + xe * w1 + xo * w2 + b, xe * w0 + xo * w1 + xe_next * w2 + b

    ze, zo = conv3(v_ref, wv_ref, bv_ref)
    gates = (conv3(x1_ref, w1_ref, b1_ref), conv3(x2_ref, w2_ref, b2_ref))
    filt = ((gr0_ref, gi0_ref, gh0_ref), (gr1_ref, gi1_ref, gh1_ref))
    two_pass = lambda hi_ref, lo_ref, rows, x: _bdot(hi_ref[rows, :], x) + _bdot(lo_ref[rows, :], x)
    for o in range(HY_ORDER):
        gr_ref, gi_ref, gh_ref = filt[o]
        zeb = ze.astype(BF16)
        zob = zo.astype(BF16)
        for r in range(0, H, rc):
            re_rows, im_rows = slice(r, r + rc), slice(H + r, H + r + rc)
            er = two_pass(fe_hi_ref, fe_lo_ref, re_rows, zeb)
            ei = two_pass(fe_hi_ref, fe_lo_ref, im_rows, zeb)
            orr = two_pass(fo_hi_ref, fo_lo_ref, re_rows, zob)
            oi = two_pass(fo_hi_ref, fo_lo_ref, im_rows, zob)
            zr, zi, zpr, zpi = er + orr, ei + oi, er - orr, oi - ei
            gr, gi, gpr, gpi = gr_ref[re_rows, :], gi_ref[re_rows, :], gr_ref[im_rows, :], gi_ref[im_rows, :]
            yr, yi = zr * gr - zi * gi, zr * gi + zi * gr
            ypr, ypi = zpr * gpr - zpi * gpi, zpr * gpi + zpi * gpr
            spec_e_ref[re_rows, :] = (yr + ypr).astype(BF16)
            spec_e_ref[im_rows, :] = (yi - ypi).astype(BF16)
            spec_o_ref[re_rows, :] = (yr - ypr).astype(BF16)
            spec_o_ref[im_rows, :] = (yi + ypi).astype(BF16)
        zhr = jnp.sum(alt * ze, axis=0, keepdims=True)
        zhi = -jnp.sum(alt * zo, axis=0, keepdims=True)
        ghr, ghi = gh_ref[0:1, :], gh_ref[1:2, :]
        yhr, yhi = zhr * ghr - zhi * ghi, zhr * ghi + zhi * ghr
        ye, yo = [], []
        for r in range(0, H, rc):
            rows = slice(r, r + rc)
            ye.append(two_pass(ie_hi_ref, ie_lo_ref, rows, spec_e_ref[...]))
            yo.append(two_pass(io_hi_ref, io_lo_ref, rows, spec_o_ref[...]))
        ye = jnp.concatenate(ye, axis=0) + alt * yhr
        yo = jnp.concatenate(yo, axis=0) - alt * yhi
        skip = skip_ref[o:o + 1, :]
        ze = gates[o][0] * (ye + skip * ze)
        zo = gates[o][1] * (yo + skip * zo)
    for c in range(n_half):
        lanes = slice(c * LANE, (c + 1) * LANE)
        out_ref[c, pl.ds(0, H, stride=2), :] = ze[:, lanes]
        out_ref[c, pl.ds(1, H, stride=2), :] = zo[:, lanes]
        o_ref[0, :, lanes] = out_ref[c].astype(BF16)


def _hyena_call(u3, L, conv_w, conv_b, spectrum, skip):
    B = u3.shape[0]
    H = L // 2
    nct = HY_WIDTH // HY_CT
    hr, hi, hn = spectrum
    mats = _dft_mats(L)
    off = HY_OFF // HY_CT
    u_spec = lambda part: pl.BlockSpec((1, L, HY_CT), lambda b, ct: (b, 0, off + part * nct + ct))
    cw_spec = lambda part: pl.BlockSpec((3, HY_CT), lambda b, ct: (0, part * nct + ct))
    cb_spec = lambda part: pl.BlockSpec((1, HY_CT), lambda b, ct: (0, part * nct + ct))
    h_spec = lambda o: pl.BlockSpec((L, HY_CT), lambda b, ct: (0, o * nct + ct))
    n_spec = lambda o: pl.BlockSpec((2, HY_CT), lambda b, ct: (0, o * nct + ct))
    cb = conv_b.reshape(1, -1)
    return pl.pallas_call(
        _hyena_kernel,
        grid=(B, nct),
        in_specs=[
            u_spec(0), u_spec(1), u_spec(2),
            cw_spec(0), cw_spec(1), cw_spec(2),
            cb_spec(0), cb_spec(1), cb_spec(2),
            h_spec(0), h_spec(0), h_spec(1), h_spec(1), n_spec(0), n_spec(1),
            pl.BlockSpec((HY_ORDER, HY_CT), lambda b, ct: (0, ct)),
            _const_spec((L, H)), _const_spec((L, H)), _const_spec((L, H)), _const_spec((L, H)),
            _const_spec((H, L)), _const_spec((H, L)), _const_spec((H, L)), _const_spec((H, L)),
        ],
        out_specs=pl.BlockSpec((1, L, HY_CT), lambda b, ct: (b, 0, ct)),
        out_shape=jax.ShapeDtypeStruct((B, L, HY_WIDTH), BF16),
        scratch_shapes=[pltpu.VMEM((L, HY_CT), BF16), pltpu.VMEM((L, HY_CT), BF16),
                        pltpu.VMEM((HY_CT // LANE, L, LANE), F32)],
        compiler_params=_cparams(("arbitrary", "arbitrary")),
        name="hyena_mix",
    )(u3, u3, u3, conv_w, conv_w, conv_w, cb, cb, cb, hr, hi, hr, hi, hn, hn, skip,
      *mats["fe"], *mats["fo"], *mats["fe_t"], *mats["fo_t"])


def _sink_softmax_pv(scores, values, sink):
    m = sink
    for s in scores:
        m = jnp.maximum(m, jnp.max(s, axis=-1, keepdims=True))
    ps = [jnp.exp(s - m) for s in scores]
    den = jnp.exp(sink - m)
    for p in ps:
        den = den + jnp.sum(p, axis=-1, keepdims=True)
    inv = 1.0 / den
    out = None
    for p, v in zip(ps, values):
        t = _bdot((p * inv).astype(BF16), v)
        out = t if out is None else out + t
    return out


def _ctx_attn_kernel(sink_ref, q_ref, k_ref, v_ref, o_ref):
    h = pl.program_id(1)
    k = k_ref[0].astype(BF16)
    v = v_ref[0].astype(BF16)
    for g in range(GQA):
        cols = slice(g * HEAD_DIM, (g + 1) * HEAD_DIM)
        q = q_ref[0, :, cols].astype(BF16)
        s = lax.dot_general(q, k, (((1,), (1,)), ((), ())), preferred_element_type=F32) * SCALE
        o = _sink_softmax_pv([s], [v], sink_ref[h * GQA + g])
        o_ref[0, :, cols] = o.astype(BF16)


def _ctx_attn_call(u3, sink):
    B, S, _ = u3.shape
    gw = GQA * HEAD_DIM
    return pl.pallas_call(
        _ctx_attn_kernel,
        grid=(B, N_KV_HEADS),
        in_specs=[
            pl.BlockSpec(memory_space=pltpu.SMEM),
            pl.BlockSpec((1, S, gw), lambda b, h: (b, 0, Q_OFF // gw + h)),
            pl.BlockSpec((1, S, HEAD_DIM), lambda b, h: (b, 0, K_OFF // HEAD_DIM + h)),
            pl.BlockSpec((1, S, HEAD_DIM), lambda b, h: (b, 0, V_OFF // HEAD_DIM + h)),
        ],
        out_specs=pl.BlockSpec((1, S, gw), lambda b, h: (b, 0, h)),
        out_shape=jax.ShapeDtypeStruct((B, S, ATT_WIDTH), BF16),
        compiler_params=_cparams(("arbitrary", "arbitrary")),
        name="context_attention",
    )(sink, u3, u3, u3)


def _rope(x, cos, sin_signed):
    half = ROPE_AXIS_DIM // 2
    width = x.shape[1]
    lane = lax.broadcasted_iota(jnp.int32, x.shape, 1)
    low_half = (lane & (ROPE_AXIS_DIM - 1)) < half
    partner = jnp.where(low_half, pltpu.roll(x, width - half, axis=1), pltpu.roll(x, half, axis=1))
    return x * cos + partner * sin_signed


def _lat_attn_kernel(sink_ref, q_ref, k_ref, v_ref, ck_ref, cv_ref, cos_ref, sin_ref, o_ref, kr_ref, vb_ref):
    h = pl.program_id(1)
    L = k_ref.shape[1]
    nb = L // BLOCK
    rows4 = GQA * BLOCK
    c2 = SCALE * math.log2(math.e)
    kr_ref[...] = _rope(k_ref[0], cos_ref[:, :HEAD_DIM], sin_ref[:, :HEAD_DIM]).astype(BF16)
    vb_ref[...] = v_ref[0].astype(BF16)
    ck = ck_ref[0, 0].astype(BF16)
    cv = cv_ref[0, 0].astype(BF16)
    nt = (((1,), (1,)), ((), ()))

    r = lax.broadcasted_iota(jnp.int32, (rows4, 1), 0)
    sink = jnp.full((rows4, 1), sink_ref[h * GQA], F32)
    for g in range(1, GQA):
        sink = jnp.where(r >= g * BLOCK, sink_ref[h * GQA + g], sink)
    sink = sink * (1.0 / SCALE)
    a = r & (BLOCK - 1)
    b = lax.broadcasted_iota(jnp.int32, (1, BLOCK), 1)
    keep_prev = b >= a
    keep_next = b <= a

    for n in range(nb):
        rows = slice(n * BLOCK, (n + 1) * BLOCK)
        qa = _rope(q_ref[0, rows, :], cos_ref[rows, :], sin_ref[rows, :]).astype(BF16)
        q = jnp.concatenate([qa[:, g * HEAD_DIM:(g + 1) * HEAD_DIM] for g in range(GQA)], axis=0)
        scores, values = [], []
        for kb, keep in ((n - 1, keep_prev), (n, None), (n + 1, keep_next)):
            if 0 <= kb < nb:
                krows = slice(kb * BLOCK, (kb + 1) * BLOCK)
                s = lax.dot_general(q, kr_ref[krows, :], nt, preferred_element_type=F32)
                scores.append(s if keep is None else jnp.where(keep, s, -1e30))
                values.append(vb_ref[krows, :])
        scores.append(lax.dot_general(q, ck, nt, preferred_element_type=F32))
        values.append(cv)
        m = sink
        for s in scores:
            m = jnp.maximum(m, jnp.max(s, axis=-1, keepdims=True))
        ps = [jnp.exp2((s - m) * c2) for s in scores]
        den = jnp.exp2((sink - m) * c2)
        for p in ps:
            den = den + jnp.sum(p, axis=-1, keepdims=True)
        inv = 1.0 / den
        o = None
        for p, v in zip(ps, values):
            t = _bdot((p * inv).astype(BF16), v)
            o = t if o is None else o + t
        for g in range(GQA):
            o_ref[0, rows, g * HEAD_DIM:(g + 1) * HEAD_DIM] = o[g * BLOCK:(g + 1) * BLOCK].astype(BF16)


def _lat_attn_call(u3, cache_k_l, cache_v_l, layer, sink, cos, sin_signed):
    B, L, _ = u3.shape
    P = cache_k_l.shape[2]
    gw = GQA * HEAD_DIM
    c_spec = pl.BlockSpec((1, 1, P, HEAD_DIM), lambda b, h: (b, layer, 0, h))
    return pl.pallas_call(
        _lat_attn_kernel,
        grid=(B, N_KV_HEADS),
        in_specs=[
            pl.BlockSpec(memory_space=pltpu.SMEM),
            pl.BlockSpec((1, L, gw), lambda b, h: (b, 0, Q_OFF // gw + h)),
            pl.BlockSpec((1, L, HEAD_DIM), lambda b, h: (b, 0, K_OFF // HEAD_DIM + h)),
            pl.BlockSpec((1, L, HEAD_DIM), lambda b, h: (b, 0, V_OFF // HEAD_DIM + h)),
            c_spec, c_spec,
            _const_spec((L, gw)), _const_spec((L, gw)),
        ],
        out_specs=pl.BlockSpec((1, L, gw), lambda b, h: (b, 0, h)),
        out_shape=jax.ShapeDtypeStruct((B, L, ATT_WIDTH), BF16),
        scratch_shapes=[pltpu.VMEM((L, HEAD_DIM), BF16), pltpu.VMEM((L, HEAD_DIM), BF16)],
        compiler_params=_cparams(("arbitrary", "arbitrary")),
        name="latent_attention",
    )(sink, u3, u3, u3, cache_k_l, cache_v_l, jnp.tile(cos, (1, GQA)), jnp.tile(sin_signed, (1, GQA)))


def _rope_tables(L):
    rows = L // GRID_W
    row_pos = jnp.repeat(jnp.arange(rows), GRID_W)
    col_pos = jnp.arange(L) % GRID_W
    half = ROPE_AXIS_DIM // 2
    inv = ROPE_THETA ** (-jnp.arange(half, dtype=F32) * 2.0 / ROPE_AXIS_DIM)
    cos_parts, sin_parts = [], []
    for pos in (row_pos, col_pos):
        ang = pos.astype(F32)[:, None] * inv[None, :]
        cos_parts += [jnp.cos(ang), jnp.cos(ang)]
        sin_parts += [-jnp.sin(ang), jnp.sin(ang)]
    return jnp.concatenate(cos_parts, axis=-1), jnp.concatenate(sin_parts, axis=-1)


def _outproj_kernel(f_ref, hy_ref, att_ref, wf_ref, wh_ref, wa_ref, o_ref):
    o_ref[...] = (_bdot(f_ref[...], wf_ref[...]) + _bdot(hy_ref[...], wh_ref[...])
                  + _bdot(att_ref[...], wa_ref[...]))


def _outproj_call(f, hy, att, w_out, layer):
    T = f.shape[0]
    assert F_WIDTH == HY_WIDTH and ATT_WIDTH == F_WIDTH + HY_WIDTH
    return pl.pallas_call(
        _outproj_kernel,
        grid=(T // TM_MM, D_MODEL // TN_MM),
        in_specs=[
            pl.BlockSpec((TM_MM, F_WIDTH), lambda i, j: (i, 0)),
            pl.BlockSpec((TM_MM, HY_WIDTH), lambda i, j: (i, 0)),
            pl.BlockSpec((TM_MM, ATT_WIDTH), lambda i, j: (i, 0)),
            pl.BlockSpec((None, F_WIDTH, TN_MM), lambda i, j: (layer, 0, j)),
            pl.BlockSpec((None, HY_WIDTH, TN_MM), lambda i, j: (layer, 1, j)),
            pl.BlockSpec((None, ATT_WIDTH, TN_MM), lambda i, j: (layer, 1, j)),
        ],
        out_specs=pl.BlockSpec((TM_MM, TN_MM), lambda i, j: (i, j)),
        out_shape=jax.ShapeDtypeStruct((T, D_MODEL), F32),
        compiler_params=_cparams(("arbitrary", "arbitrary")),
        name="outproj",
    )(f, hy, att, w_out, w_out, w_out)


def _ffn_kernel(seq_len, nf, h_ref, wg_ref, wv_ref, cwg_ref, cwv_ref, cbg_ref, cbv_ref, wd_ref, o_ref,
                u_ref, act_ref):
    g = pl.program_id(0)
    tm = h_ref.shape[0]
    seg = min(seq_len, tm)
    n_seg = tm // seg
    base = [SUBLANE + s * (seg + SUBLANE) for s in range(n_seg)]

    @pl.when(g == 0)
    def _():
        u_ref[...] = jnp.zeros(u_ref.shape, F32)
        act_ref[...] = jnp.zeros(act_ref.shape, BF16)

    @pl.when((g == 0) | (lax.rem(g + (nf - 2), nf) == 0))
    def _():
        o_ref[...] = jnp.zeros(o_ref.shape, F32)

    live = (g > 0).astype(F32)

    def conv_piece(slot, p):
        rp = tm // FFN_CONV_PIECES
        s, r0 = divmod(p * rp, seg)
        lo = base[s] + r0

        def conv(part, cw_ref, cb_ref):
            win = lambda off: u_ref[1 - slot, part, lo + off:lo + off + rp, :]
            return win(-1) * cw_ref[0:1, :] + win(0) * cw_ref[1:2, :] + win(1) * cw_ref[2:3, :] + cb_ref[...]

        gate = conv(0, cwg_ref, cbg_ref)
        val = conv(1, cwv_ref, cbv_ref)
        act_ref[1 - slot, p * rp:(p + 1) * rp, :] = (_silu(gate) * (val * live)).astype(BF16)

    def down_cols(slot, p):
        cp = D_MODEL // FFN_PIECES
        cols = slice(p * cp, (p + 1) * cp)
        o_ref[:, cols] += _bdot(act_ref[slot], wd_ref[:, cols].astype(BF16))

    def up_rows(slot, part, r):
        w_ref = (wg_ref, wv_ref)[part]
        s, r0 = divmod(r * up_rb, seg)
        u_ref[slot, part, base[s] + r0:base[s] + r0 + up_rb, :] = _bdot(
            h_ref[r * up_rb:(r + 1) * up_rb, :], w_ref[...].astype(BF16))

    up_rb = min(seg, FFN_UP_ROWS)
    segments = [(functools.partial(down_cols, p=p), FFN_DOWN_SHARE * tm * (D_MODEL // FFN_PIECES))
                for p in range(FFN_PIECES)]
    segments += [(functools.partial(up_rows, part=part, r=r), up_rb * D_MODEL)
                 for part in (0, 1) for r in range(tm // up_rb)]
    total_time = sum(t for _, t in segments[:-1])

    def step(slot):
        done, emitted = 0, 0
        for run, t in segments:
            while emitted < FFN_CONV_PIECES and emitted * total_time <= done * FFN_CONV_PIECES:
                conv_piece(slot, emitted)
                emitted += 1
            run(slot)
            done += t
        for p in range(emitted, FFN_CONV_PIECES):
            conv_piece(slot, p)

    for slot in (0, 1):
        pl.when((g & 1) == slot)(functools.partial(step, slot))


def _ffn_call(h, seq_len, w_up, w_up_val, conv_w, conv_b, w_down, layer):
    T = h.shape[0]
    nf = D_FF // TF_FFN
    n_tiles = T // TM_FFN
    assert TM_FFN % seq_len == 0
    cb = conv_b.reshape(DEPTH, 1, -1)
    wspec = lambda shape, imap: pl.BlockSpec((None,) + shape, imap)
    seg = min(seq_len, TM_FFN)
    u_rows = SUBLANE + (TM_FFN // seg) * (seg + SUBLANE)
    up_chunk = lambda g: lax.rem(g, nf)
    cv_chunk = lambda g: lax.rem(g + (nf - 1), nf)
    dn_chunk = lambda g: lax.rem(g + (nf - 2), nf)
    up_tile = lambda g: jnp.minimum(g // nf, n_tiles - 1)
    dn_tile = lambda g: jnp.clip((g + (nf - 2)) // nf - 1, 0, n_tiles - 1)
    return pl.pallas_call(
        functools.partial(_ffn_kernel, seq_len, nf),
        grid=(n_tiles * nf + 2,),
        in_specs=[
            pl.BlockSpec((TM_FFN, D_MODEL), lambda g: (up_tile(g), 0), pipeline_mode=pl.Buffered(1)),
            wspec((D_MODEL, TF_FFN), lambda g: (layer, 0, up_chunk(g))),
            wspec((D_MODEL, TF_FFN), lambda g: (layer, 0, up_chunk(g))),
            wspec((3, TF_FFN), lambda g: (layer, 0, cv_chunk(g))),
            wspec((3, TF_FFN), lambda g: (layer, 0, nf + cv_chunk(g))),
            wspec((1, TF_FFN), lambda g: (layer, 0, cv_chunk(g))),
            wspec((1, TF_FFN), lambda g: (layer, 0, nf + cv_chunk(g))),
            wspec((TF_FFN, D_MODEL), lambda g: (layer, dn_chunk(g), 0)),
        ],
        out_specs=pl.BlockSpec((TM_FFN, D_MODEL), lambda g: (dn_tile(g), 0), pipeline_mode=pl.Buffered(1)),
        out_shape=jax.ShapeDtypeStruct((T, D_MODEL), F32),
        scratch_shapes=[pltpu.VMEM((2, 2, u_rows, TF_FFN), F32), pltpu.VMEM((2, TM_FFN, TF_FFN), BF16)],
        compiler_params=_cparams(("arbitrary",)),
        name="conv_ffn",
    )(h, w_up, w_up_val, conv_w, conv_w, cb, cb, w_down)


def _layer(x, h, B, L, layer, mods, modrow_of_tile, big, p, attend):
    T = B * L
    u3 = _inproj_call(h, big["w_in"], layer).reshape(B, L, IN_WIDTH)
    f = _fourier_call(u3, L)
    spectrum = _filters_call(L, p["flt_w1"], p["flt_b1"], p["flt_w2"], p["flt_b2"], p["flt_w3"], p["flt_freq"])
    hy = _hyena_call(u3, L, p["hy_conv_w"], p["hy_conv_b"], spectrum, p["hy_skip"])
    att = attend(u3)
    proj = _outproj_call(f.reshape(T, F_WIDTH), hy.reshape(T, HY_WIDTH), att.reshape(T, ATT_WIDTH),
                         big["w_out"], layer)
    x, h2 = _resid_ln_call(x, proj, mods, layer, 2, big["ln1_g"], big["ln1_b"], modrow_of_tile,
                           h_layer=layer, h_row=3)
    ff = _ffn_call(h2, L, big["ffn_w_up"], big["ffn_w_up_val"], big["ffn_conv_w"], big["ffn_conv_b"],
                   big["ffn_w_down"], layer)
    if layer + 1 < DEPTH:
        x, h_next = _resid_ln_call(x, ff, mods, layer, 5, big["ln2_g"], big["ln2_b"], modrow_of_tile,
                                   h_layer=layer + 1, h_row=0)
    else:
        x, h_next = _resid_ln_call(x, ff, mods, layer, 5, big["ln2_g"], big["ln2_b"], modrow_of_tile)
    return x, h_next, u3


def kernel(x_prompt, x_sample, cache_k, cache_v, c, c_ctx, w_mod, b_mod, w_in, w_out, attn_sink,
           hy_conv_w, hy_conv_b, flt_w1, flt_b1, flt_w2, flt_b2, flt_w3, flt_freq, hy_skip,
           ln1_g, ln1_b, ffn_w_up, ffn_conv_w, ffn_conv_b, ffn_w_down, ln2_g, ln2_b):
    Bp, S, _ = x_prompt.shape
    Bs, Ls, _ = x_sample.shape
    past = cache_k.shape[2]

    cvec = jnp.zeros((MOD_ROWS, D_MODEL), F32).at[0].set(c_ctx).at[1:1 + Bs].set(c)
    mods = _mod_call(cvec, w_mod, b_mod).reshape(DEPTH, MOD_ROWS, 6, D_MODEL)

    big = dict(w_in=w_in.astype(BF16), w_out=w_out.astype(BF16), ffn_w_up=ffn_w_up,
               ffn_w_up_val=ffn_w_up[:, :, D_FF:].astype(BF16),
               ffn_w_down=ffn_w_down, ffn_conv_w=ffn_conv_w, ffn_conv_b=ffn_conv_b,
               ln1_g=ln1_g, ln1_b=ln1_b, ln2_g=ln2_g, ln2_b=ln2_b)
    ck = cache_k.reshape(Bs, DEPTH, past, KV_WIDTH)
    cv = cache_v.reshape(Bs, DEPTH, past, KV_WIDTH)
    cos, sin_signed = _rope_tables(Ls)

    ctx_row = lambda i, tm: 0
    lat_row = lambda i, tm: 1 + i * tm // Ls

    xp = x_prompt.reshape(Bp * S, D_MODEL)
    xs = x_sample.reshape(Bs * Ls, D_MODEL)
    hp = _ln_mod_call(xp, mods, 0, ctx_row)
    hs = _ln_mod_call(xs, mods, 0, lat_row)
    ctx_k, ctx_v = [], []
    for l in range(DEPTH):
        p = dict(flt_w1=flt_w1[l], flt_b1=flt_b1[l], flt_w2=flt_w2[l], flt_b2=flt_b2[l], flt_w3=flt_w3[l],
                 flt_freq=flt_freq[l], hy_conv_w=hy_conv_w[l], hy_conv_b=hy_conv_b[l], hy_skip=hy_skip[l])
        sink = attn_sink[l]
        xp, hp, up3 = _layer(xp, hp, Bp, S, l, mods, ctx_row, big, p, lambda u3: _ctx_attn_call(u3, sink))
        ctx_k.append(up3[:, :, K_OFF:K_OFF + KV_WIDTH].reshape(Bp, S, N_KV_HEADS, HEAD_DIM))
        ctx_v.append(up3[:, :, V_OFF:V_OFF + KV_WIDTH].reshape(Bp, S, N_KV_HEADS, HEAD_DIM))
        xs, hs, _ = _layer(xs, hs, Bs, Ls, l, mods, lat_row, big, p,
                           lambda u3: _lat_attn_call(u3, ck, cv, l, sink, cos, sin_signed))

    return (xp.reshape(Bp, S, D_MODEL), xs.reshape(Bs, Ls, D_MODEL),
            jnp.stack(ctx_k, axis=1), jnp.stack(ctx_v, axis=1))
```

```python
import functools
import math

import ml_dtypes
import numpy as np
import jax
import jax.numpy as jnp
from jax import lax
from jax.experimental import pallas as pl
from jax.experimental.pallas import tpu as pltpu

F32 = jnp.float32
BF16 = jnp.bfloat16

D_MODEL = 4096
DEPTH = 2
GRID_W = 64
F_WIDTH = D_MODEL // 4
F_GROUPS = 8
F_GROUP_W = F_WIDTH // F_GROUPS
HY_WIDTH = D_MODEL // 4
HY_ORDER = 2
N_BANDS = 16
POS_EMB = 1 + 2 * N_BANDS
FILTER_HID = 64
DECAY_TARGET = 1e-2
MIN_DECAY = math.log(DECAY_TARGET) / 1.5
MAX_DECAY = math.log(DECAY_TARGET) / 0.3
HEAD_DIM = 128
N_Q_HEADS = (D_MODEL // 2) // HEAD_DIM
N_KV_HEADS = N_Q_HEADS // 4
GQA = N_Q_HEADS // N_KV_HEADS
ATT_WIDTH = N_Q_HEADS * HEAD_DIM
KV_WIDTH = N_KV_HEADS * HEAD_DIM
WINDOW = 128
BLOCK = 128
SCALE = HEAD_DIM ** -0.5
ROPE_THETA = 10000.0
ROPE_AXIS_DIM = HEAD_DIM // 2
D_FF = 256 * ((8 * D_MODEL // 3 + 255) // 256)
LN_EPS = 1e-5
ALPHA = (2 * DEPTH) ** 0.25
IN_WIDTH = F_WIDTH + 3 * HY_WIDTH + ATT_WIDTH + 2 * KV_WIDTH
MIX_WIDTH = F_WIDTH + HY_WIDTH + ATT_WIDTH

HY_OFF = F_WIDTH
Q_OFF = F_WIDTH + 3 * HY_WIDTH
K_OFF = Q_OFF + ATT_WIDTH
V_OFF = K_OFF + KV_WIDTH

MOD_ROWS = 16
LANE = 128
SUBLANE = 8
VMEM_LIMIT = 56 * 1024 * 1024

BF16_ROWS = 16
TM_MM = 1024
TN_MM = 512
TM_LN = 256
TM_FFN = 1024
TF_FFN = 256
FFN_UP_ROWS = 256
FFN_PIECES = 16
FFN_DOWN_SHARE = 2
FFN_CONV_PIECES = 16
HY_CT = 256
TN_MOD = 512
TK_MOD = 512
ROW_CHUNK = 32


def _cparams(sem):
    return pltpu.CompilerParams(dimension_semantics=sem, vmem_limit_bytes=VMEM_LIMIT)


def _const_spec(shape):
    return pl.BlockSpec(shape, lambda *_: (0,) * len(shape), pipeline_mode=pl.Buffered(1))


def _bdot(a, b):
    return jnp.dot(a, b, preferred_element_type=F32)


def _ln_rows(x):
    mu = jnp.mean(x, axis=-1, keepdims=True)
    xc = x - mu
    var = jnp.mean(xc * xc, axis=-1, keepdims=True)
    return xc * lax.rsqrt(var + LN_EPS)


def _silu(x):
    return x / (1.0 + jnp.exp(-x))


def _for_row_chunks(n_rows, body):
    def step(c, carry):
        body(pl.ds(pl.multiple_of(c * ROW_CHUNK, ROW_CHUNK), ROW_CHUNK))
        return carry
    lax.fori_loop(0, n_rows // ROW_CHUNK, step, 0)


def _hi_lo(x):
    hi = x.astype(ml_dtypes.bfloat16)
    lo = (x - hi.astype(np.float64)).astype(ml_dtypes.bfloat16)
    return jnp.asarray(hi), jnp.asarray(lo)


@functools.lru_cache(maxsize=None)
def _dft_mats(L):
    H = L // 2
    k = np.arange(H)
    t = np.arange(H)
    ang_e = 2.0 * np.pi * ((k[:, None] * t[None, :]) % L) / L
    fe = np.concatenate([np.cos(ang_e), -np.sin(ang_e)], axis=0)
    ang_o = np.pi * ((k[:, None] * (2 * t[None, :] + 1)) % (2 * L)) / L
    fo = np.concatenate([np.cos(ang_o), -np.sin(ang_o)], axis=0)
    bins = np.concatenate([k, L - k])
    n = np.arange(L)
    ang_s = np.pi * ((bins[:, None] * n[None, :]) % (2 * L)) / L
    wgt = np.where(np.concatenate([k, k]) == 0, 1.0, 2.0)[:, None] / (2 * L)
    fsel = np.concatenate([np.cos(ang_s) * wgt, -np.sin(ang_s) * wgt], axis=0)
    ang_me = 2.0 * np.pi * ((k[:, None] * (2 * t[None, :])) % L) / L
    ang_mo = 2.0 * np.pi * ((k[:, None] * (2 * t[None, :] + 1)) % L) / L
    fme = np.concatenate([np.cos(ang_me), -np.sin(ang_me)], axis=1) / np.sqrt(L)
    fmo = np.concatenate([np.cos(ang_mo), -np.sin(ang_mo)], axis=1) / np.sqrt(L)
    return dict(fe=_hi_lo(fe), fo=_hi_lo(fo), fe_t=_hi_lo(fe.T), fo_t=_hi_lo(fo.T), fsel=_hi_lo(fsel),
                fme=_hi_lo(fme), fmo=_hi_lo(fmo))


@functools.lru_cache(maxsize=None)
def _chan_dft():
    n = np.arange(F_GROUP_W)
    ang = 2.0 * np.pi * ((n[:, None] * n[None, :]) % F_GROUP_W) / F_GROUP_W
    return _hi_lo(np.concatenate([np.cos(ang), np.sin(ang)], axis=1) / np.sqrt(F_GROUP_W))


def _mod_kernel(c_ref, w_ref, b_ref, o_ref):
    acc = b_ref[0] + jnp.zeros((MOD_ROWS, TN_MOD), F32)
    for k in range(0, D_MODEL, TK_MOD):
        s = _silu(c_ref[:, k:k + TK_MOD]).astype(BF16)
        acc = acc + _bdot(s, w_ref[0, k:k + TK_MOD, :].astype(BF16))
    o_ref[0] = acc


def _mod_call(cvec, w_mod, b_mod):
    n_out = w_mod.shape[-1]
    return pl.pallas_call(
        _mod_kernel,
        grid=(DEPTH, n_out // TN_MOD),
        in_specs=[
            pl.BlockSpec((MOD_ROWS, D_MODEL), lambda l, j: (0, 0)),
            pl.BlockSpec((1, D_MODEL, TN_MOD), lambda l, j: (l, 0, j)),
            pl.BlockSpec((1, 1, TN_MOD), lambda l, j: (l, 0, j)),
        ],
        out_specs=pl.BlockSpec((1, MOD_ROWS, TN_MOD), lambda l, j: (l, 0, j)),
        out_shape=jax.ShapeDtypeStruct((DEPTH, MOD_ROWS, n_out), F32),
        compiler_params=_cparams(("arbitrary", "arbitrary")),
        name="adaln_mod",
    )(cvec, w_mod, b_mod.reshape(DEPTH, 1, n_out))


def _inproj_kernel(h_ref, w_ref, o_ref):
    o_ref[...] = _bdot(h_ref[...], w_ref[...].astype(BF16))


def _inproj_call(h, w_in, layer):
    T = h.shape[0]
    return pl.pallas_call(
        _inproj_kernel,
        grid=(T // TM_MM, IN_WIDTH // TN_MM),
        in_specs=[
            pl.BlockSpec((TM_MM, D_MODEL), lambda i, j: (i, 0)),
            pl.BlockSpec((None, D_MODEL, TN_MM), lambda i, j: (layer, 0, j)),
        ],
        out_specs=pl.BlockSpec((TM_MM, TN_MM), lambda i, j: (i, j)),
        out_shape=jax.ShapeDtypeStruct((T, IN_WIDTH), F32),
        compiler_params=_cparams(("arbitrary", "arbitrary")),
        name="inproj",
    )(h, w_in)


def _mod_spec(layer, modrow_of_tile, tm):
    return pl.BlockSpec((None, None, 6, D_MODEL), lambda i: (layer, modrow_of_tile(i, tm), 0, 0))


def _ln_mod_kernel(x_ref, mod_ref, h_ref):
    def body(rows):
        h = _ln_rows(x_ref[rows, :]) * (1.0 + mod_ref[1:2, :]) + mod_ref[0:1, :]
        h_ref[rows, :] = h.astype(BF16)
    _for_row_chunks(TM_LN, body)


def _ln_mod_call(x, mods, layer, modrow_of_tile):
    T = x.shape[0]
    return pl.pallas_call(
        _ln_mod_kernel,
        grid=(T // TM_LN,),
        in_specs=[pl.BlockSpec((TM_LN, D_MODEL), lambda i: (i, 0)), _mod_spec(layer, modrow_of_tile, TM_LN)],
        out_specs=pl.BlockSpec((TM_LN, D_MODEL), lambda i: (i, 0)),
        out_shape=jax.ShapeDtypeStruct((T, D_MODEL), BF16),
        compiler_params=_cparams(("arbitrary",)),
        name="ln_modulate",
    )(x, mods)


def _resid_ln_kernel(gate_row, h_row, x_ref, d_ref, modg_ref, g_ref, b_ref, *rest):
    if h_row is None:
        (xo_ref,) = rest
    else:
        modh_ref, xo_ref, ho_ref = rest

    def body(rows):
        y = ALPHA * x_ref[rows, :] + modg_ref[gate_row:gate_row + 1, :] * d_ref[rows, :]
        xn = _ln_rows(y) * g_ref[...] + b_ref[...]
        xo_ref[rows, :] = xn
        if h_row is not None:
            h = _ln_rows(xn) * (1.0 + modh_ref[h_row + 1:h_row + 2, :]) + modh_ref[h_row:h_row + 1, :]
            ho_ref[rows, :] = h.astype(BF16)
    _for_row_chunks(TM_LN, body)


def _resid_ln_call(x, d, mods, layer, gate_row, ln_g, ln_b, modrow_of_tile, h_layer=None, h_row=None):
    T = x.shape[0]
    tile = pl.BlockSpec((TM_LN, D_MODEL), lambda i: (i, 0))
    vec = pl.BlockSpec((None, 1, D_MODEL), lambda i: (layer, 0, 0))
    in_specs = [tile, tile, _mod_spec(layer, modrow_of_tile, TM_LN), vec, vec]
    args = [x, d, mods, ln_g.reshape(DEPTH, 1, D_MODEL), ln_b.reshape(DEPTH, 1, D_MODEL)]
    out_specs = [tile]
    out_shape = [jax.ShapeDtypeStruct((T, D_MODEL), F32)]
    if h_row is not None:
        in_specs.append(_mod_spec(h_layer, modrow_of_tile, TM_LN))
        args.append(mods)
        out_specs.append(tile)
        out_shape.append(jax.ShapeDtypeStruct((T, D_MODEL), BF16))
    out = pl.pallas_call(
        functools.partial(_resid_ln_kernel, gate_row, h_row),
        grid=(T // TM_LN,),
        in_specs=in_specs,
        out_specs=out_specs,
        out_shape=out_shape,
        compiler_params=_cparams(("arbitrary",)),
        name="residual_ln",
    )(*args)
    return out if h_row is not None else (out[0], None)


def _fourier_kernel(u_ref, cc_hi_ref, cc_lo_ref, fme_hi_ref, fme_lo_ref, fmo_hi_ref, fmo_lo_ref, o_ref,
                    pe_ref, po_ref, rows_ref):
    L = u_ref.shape[1]
    H = L // 2
    rc = min(H, 256)
    for g in range(F_GROUPS):
        cols = slice(g * F_GROUP_W, (g + 1) * F_GROUP_W)
        rows_ref[...] = u_ref[0, :, cols]
        for parity, p_ref in enumerate((pe_ref, po_ref)):
            ug = rows_ref[pl.ds(parity, H, stride=2), :].astype(BF16)
            a = _bdot(ug, cc_hi_ref[...]) + _bdot(ug, cc_lo_ref[...])
            p_ref[0:H, cols] = a[:, :F_GROUP_W].astype(BF16)
            p_ref[H:L, cols] = a[:, F_GROUP_W:].astype(BF16)
    for r in range(0, H, rc):
        rows = slice(r, r + rc)
        e = _bdot(fme_hi_ref[rows, :], pe_ref[...]) + _bdot(fme_lo_ref[rows, :], pe_ref[...])
        o = _bdot(fmo_hi_ref[rows, :], po_ref[...]) + _bdot(fmo_lo_ref[rows, :], po_ref[...])
        o_ref[0, r:r + rc, :] = (e + o).astype(BF16)
        o_ref[0, H + r:H + r + rc, :] = (e - o).astype(BF16)


def _fourier_call(u3, L):
    B = u3.shape[0]
    H = L // 2
    mats = _dft_mats(L)
    cc_hi, cc_lo = _chan_dft()
    return pl.pallas_call(
        _fourier_kernel,
        grid=(B,),
        in_specs=[
            pl.BlockSpec((1, L, F_WIDTH), lambda b: (b, 0, 0)),
            _const_spec((F_GROUP_W, 2 * F_GROUP_W)),
            _const_spec((F_GROUP_W, 2 * F_GROUP_W)),
            _const_spec((H, L)), _const_spec((H, L)), _const_spec((H, L)), _const_spec((H, L)),
        ],
        out_specs=pl.BlockSpec((1, L, F_WIDTH), lambda b: (b, 0, 0)),
        out_shape=jax.ShapeDtypeStruct((B, L, F_WIDTH), BF16),
        scratch_shapes=[pltpu.VMEM((L, F_WIDTH), BF16), pltpu.VMEM((L, F_WIDTH), BF16),
                        pltpu.VMEM((L, F_GROUP_W), F32)],
        compiler_params=_cparams(("arbitrary",)),
        name="fourier_mix",
    )(u3, cc_hi, cc_lo, *mats["fme"], *mats["fmo"])


def _filter_taps_kernel(feat_ref, w1_ref, b1_ref, w2_ref, b2_ref, freq_ref, w3f_ref, w3b_ref, decay_ref,
                        fwd_ref, bsh_ref, h_ref):
    hp = lax.Precision.HIGHEST
    feats = feat_ref[...]

    @pl.when((pl.program_id(0) == 0) & (pl.program_id(1) == 0))
    def _():
        freq = freq_ref[...]
        h1 = jnp.sin(freq * (jnp.dot(feats, w1_ref[...], precision=hp, preferred_element_type=F32) + b1_ref[...]))
        h_ref[...] = jnp.sin(freq * (jnp.dot(h1, w2_ref[...], precision=hp, preferred_element_type=F32)
                                     + b2_ref[...]))

    h = h_ref[...]
    win = jnp.exp(-feats[:, 0:1] * decay_ref[...])
    fwd = jnp.dot(h, w3f_ref[...], precision=hp, preferred_element_type=F32) * win
    bwd = jnp.dot(h, w3b_ref[...], precision=hp, preferred_element_type=F32) * win
    row = lax.broadcasted_iota(jnp.int32, bwd.shape, 0)
    bsh = jnp.where(row == 0, 0.0, pltpu.roll(bwd, 1, axis=0))
    norm = jnp.sum(jnp.abs(fwd), axis=0, keepdims=True) + jnp.sum(jnp.abs(bsh), axis=0, keepdims=True)
    fwd_ref[...] = fwd / norm
    bsh_ref[...] = bsh / norm


def _filter_spectrum_kernel(fwd_ref, bsh_ref, f_hi_ref, f_lo_ref, gr_ref, gi_ref, gh_ref):
    L = fwd_ref.shape[0]
    fwd = fwd_ref[...]
    bsh = bsh_ref[...]
    both = jnp.concatenate([fwd, bsh], axis=1)
    b_hi = both.astype(BF16)
    b_lo = (both - b_hi.astype(F32)).astype(BF16)
    ct = fwd.shape[1]
    rc = min(L, 256)
    for r in range(0, L, rc):
        def spec(rows):
            return (_bdot(f_hi_ref[rows, :], b_hi) + _bdot(f_lo_ref[rows, :], b_hi)
                    + _bdot(f_hi_ref[rows, :], b_lo))
        c = spec(slice(r, r + rc))
        s = spec(slice(L + r, L + r + rc))
        gr_ref[r:r + rc, :] = c[:, :ct] + c[:, ct:]
        gi_ref[r:r + rc, :] = s[:, :ct] - s[:, ct:]
    q = lax.broadcasted_iota(jnp.int32, fwd.shape, 0) & 3
    c4 = jnp.where(q == 0, 1.0, jnp.where(q == 2, -1.0, 0.0))
    s4 = jnp.where(q == 1, 1.0, jnp.where(q == 3, -1.0, 0.0))
    gh_ref[0:1, :] = jnp.sum(c4 * (fwd + bsh), axis=0, keepdims=True) * (1.0 / L)
    gh_ref[1:2, :] = jnp.sum(s4 * (bsh - fwd), axis=0, keepdims=True) * (1.0 / L)


def _filters_call(L, w1, b1, w2, b2, w3, freq):
    t = np.arange(L, dtype=np.float32)
    t_norm = t / np.float32(max(L - 1, 1))
    w = np.float32(2.0 * math.pi) * t / np.float32(L)
    bands = np.linspace(1e-4, N_BANDS - 1, N_BANDS, dtype=np.float32)
    feats = np.zeros((L, LANE), np.float32)
    feats[:, 0] = t_norm
    feats[:, 1:1 + N_BANDS] = np.cos(w[:, None] * bands[None])
    feats[:, 1 + N_BANDS:POS_EMB] = np.sin(w[:, None] * bands[None])
    decay = np.abs(np.linspace(MIN_DECAY, MAX_DECAY, HY_WIDTH, dtype=np.float32))[None, :]
    w1p = jnp.zeros((LANE, FILTER_HID), F32).at[:POS_EMB].set(w1)
    nct = HY_WIDTH // HY_CT
    ocw = HY_ORDER * HY_WIDTH
    row = lambda v: v.reshape(1, -1)
    small = lambda shape: pl.BlockSpec(shape, lambda o, ct: (0, 0))
    fwd, bsh = pl.pallas_call(
        _filter_taps_kernel,
        grid=(HY_ORDER, nct),
        in_specs=[
            small((L, LANE)), small((LANE, FILTER_HID)), small((1, FILTER_HID)),
            small((FILTER_HID, FILTER_HID)), small((1, FILTER_HID)), small((1, FILTER_HID)),
            pl.BlockSpec((FILTER_HID, HY_CT), lambda o, ct: (0, o * 2 * nct + ct)),
            pl.BlockSpec((FILTER_HID, HY_CT), lambda o, ct: (0, o * 2 * nct + nct + ct)),
            pl.BlockSpec((1, HY_CT), lambda o, ct: (0, ct)),
        ],
        out_specs=[pl.BlockSpec((L, HY_CT), lambda o, ct: (0, o * nct + ct))] * 2,
        out_shape=[jax.ShapeDtypeStruct((L, ocw), F32)] * 2,
        scratch_shapes=[pltpu.VMEM((L, FILTER_HID), F32)],
        compiler_params=_cparams(("arbitrary", "arbitrary")),
        name="hyena_filter_taps",
    )(jnp.asarray(feats), w1p, row(b1), w2, row(b2), row(freq), w3, w3, jnp.asarray(decay))
    f_hi, f_lo = _dft_mats(L)["fsel"]
    return pl.pallas_call(
        _filter_spectrum_kernel,
        grid=(ocw // HY_CT,),
        in_specs=[
            pl.BlockSpec((L, HY_CT), lambda j: (0, j)),
            pl.BlockSpec((L, HY_CT), lambda j: (0, j)),
            _const_spec((2 * L, L)),
            _const_spec((2 * L, L)),
        ],
        out_specs=[pl.BlockSpec((L, HY_CT), lambda j: (0, j)),
                   pl.BlockSpec((L, HY_CT), lambda j: (0, j)),
                   pl.BlockSpec((2, HY_CT), lambda j: (0, j))],
        out_shape=[jax.ShapeDtypeStruct((L, ocw), F32), jax.ShapeDtypeStruct((L, ocw), F32),
                   jax.ShapeDtypeStruct((2, ocw), F32)],
        compiler_params=_cparams(("arbitrary",)),
        name="hyena_filter_spectrum",
    )(fwd, bsh, f_hi, f_lo)


def _hyena_kernel(v_ref, x1_ref, x2_ref, wv_ref, w1_ref, w2_ref, bv_ref, b1_ref, b2_ref,
                  gr0_ref, gi0_ref, gr1_ref, gi1_ref, gh0_ref, gh1_ref, skip_ref,
                  fe_hi_ref, fe_lo_ref, fo_hi_ref, fo_lo_ref, ie_hi_ref, ie_lo_ref, io_hi_ref, io_lo_ref,
                  o_ref, spec_e_ref, spec_o_ref, out_ref):
    L = v_ref.shape[1]
    H = L // 2
    rc = min(H, 256)
    row = lax.broadcasted_iota(jnp.int32, (H, HY_CT), 0)
    first = row == 0
    last = row == H - 1
    alt = jnp.where((row & 1) == 0, 1.0, -1.0)

    n_half = HY_CT // LANE

    def conv3(x_ref, w_ref, b_ref):
        for c in range(n_half):
            out_ref[c] = x_ref[0, :, c * LANE:(c + 1) * LANE]
        xe = jnp.concatenate([out_ref[c, pl.ds(0, H, stride=2), :] for c in range(n_half)], axis=1)
        xo = jnp.concatenate([out_ref[c, pl.ds(1, H, stride=2), :] for c in range(n_half)], axis=1)
        xo_prev = jnp.where(first, 0.0, pltpu.roll(xo, 1, axis=0))
        xe_next = jnp.where(last, 0.0, pltpu.roll(xe, H - 1, axis=0))
        w0, w1, w2, b = w_ref[0:1, :], w_ref[1:2, :], w_ref[2:3, :], b_ref[...]
        return xo_prev * w0 + xe * w1 + xo * w2 + b, xe * w0 + xo * w1 + xe_next * w2 + b

    ze, zo = conv3(v_ref, wv_ref, bv_ref)
    gates = (conv3(x1_ref, w1_ref, b1_ref), conv3(x2_ref, w2_ref, b2_ref))
    filt = ((gr0_ref, gi0_ref, gh0_ref), (gr1_ref, gi1_ref, gh1_ref))
    two_pass = lambda hi_ref, lo_ref, rows, x: _bdot(hi_ref[rows, :], x) + _bdot(lo_ref[rows, :], x)
    for o in range(HY_ORDER):
        gr_ref, gi_ref, gh_ref = filt[o]
        zeb = ze.astype(BF16)
        zob = zo.astype(BF16)
        for r in range(0, H, rc):
            re_rows, im_rows = slice(r, r + rc), slice(H + r, H + r + rc)
            er = two_pass(fe_hi_ref, fe_lo_ref, re_rows, zeb)
            ei = two_pass(fe_hi_ref, fe_lo_ref, im_rows, zeb)
            orr = two_pass(fo_hi_ref, fo_lo_ref, re_rows, zob)
            oi = two_pass(fo_hi_ref, fo_lo_ref, im_rows, zob)
            zr, zi, zpr, zpi = er + orr, ei + oi, er - orr, oi - ei
            gr, gi, gpr, gpi = gr_ref[re_rows, :], gi_ref[re_rows, :], gr_ref[im_rows, :], gi_ref[im_rows, :]
            yr, yi = zr * gr - zi * gi, zr * gi + zi * gr
            ypr, ypi = zpr * gpr - zpi * gpi, zpr * gpi + zpi * gpr
            spec_e_ref[re_rows, :] = (yr + ypr).astype(BF16)
            spec_e_ref[im_rows, :] = (yi - ypi).astype(BF16)
            spec_o_ref[re_rows, :] = (yr - ypr).astype(BF16)
            spec_o_ref[im_rows, :] = (yi + ypi).astype(BF16)
        zhr = jnp.sum(alt * ze, axis=0, keepdims=True)
        zhi = -jnp.sum(alt * zo, axis=0, keepdims=True)
        ghr, ghi = gh_ref[0:1, :], gh_ref[1:2, :]
        yhr, yhi = zhr * ghr - zhi * ghi, zhr * ghi + zhi * ghr
        ye, yo = [], []
        for r in range(0, H, rc):
            rows = slice(r, r + rc)
            ye.append(two_pass(ie_hi_ref, ie_lo_ref, rows, spec_e_ref[...]))
            yo.append(two_pass(io_hi_ref, io_lo_ref, rows, spec_o_ref[...]))
        ye = jnp.concatenate(ye, axis=0) + alt * yhr
        yo = jnp.concatenate(yo, axis=0) - alt * yhi
        skip = skip_ref[o:o + 1, :]
        ze = gates[o][0] * (ye + skip * ze)
        zo = gates[o][1] * (yo + skip * zo)
    for c in range(n_half):
        lanes = slice(c * LANE, (c + 1) * LANE)
        out_ref[c, pl.ds(0, H, stride=2), :] = ze[:, lanes]
        out_ref[c, pl.ds(1, H, stride=2), :] = zo[:, lanes]
        o_ref[0, :, lanes] = out_ref[c].astype(BF16)


def _hyena_call(u3, L, conv_w, conv_b, spectrum, skip):
    B = u3.shape[0]
    H = L // 2
    nct = HY_WIDTH // HY_CT
    hr, hi, hn = spectrum
    mats = _dft_mats(L)
    off = HY_OFF // HY_CT
    u_spec = lambda part: pl.BlockSpec((1, L, HY_CT), lambda b, ct: (b, 0, off + part * nct + ct))
    cw_spec = lambda part: pl.BlockSpec((3, HY_CT), lambda b, ct: (0, part * nct + ct))
    cb_spec = lambda part: pl.BlockSpec((1, HY_CT), lambda b, ct: (0, part * nct + ct))
    h_spec = lambda o: pl.BlockSpec((L, HY_CT), lambda b, ct: (0, o * nct + ct))
    n_spec = lambda o: pl.BlockSpec((2, HY_CT), lambda b, ct: (0, o * nct + ct))
    cb = conv_b.reshape(1, -1)
    return pl.pallas_call(
        _hyena_kernel,
        grid=(B, nct),
        in_specs=[
            u_spec(0), u_spec(1), u_spec(2),
            cw_spec(0), cw_spec(1), cw_spec(2),
            cb_spec(0), cb_spec(1), cb_spec(2),
            h_spec(0), h_spec(0), h_spec(1), h_spec(1), n_spec(0), n_spec(1),
            pl.BlockSpec((HY_ORDER, HY_CT), lambda b, ct: (0, ct)),
            _const_spec((L, H)), _const_spec((L, H)), _const_spec((L, H)), _const_spec((L, H)),
            _const_spec((H, L)), _const_spec((H, L)), _const_spec((H, L)), _const_spec((H, L)),
        ],
        out_specs=pl.BlockSpec((1, L, HY_CT), lambda b, ct: (b, 0, ct)),
        out_shape=jax.ShapeDtypeStruct((B, L, HY_WIDTH), BF16),
        scratch_shapes=[pltpu.VMEM((L, HY_CT), BF16), pltpu.VMEM((L, HY_CT), BF16),
                        pltpu.VMEM((HY_CT // LANE, L, LANE), F32)],
        compiler_params=_cparams(("arbitrary", "arbitrary")),
        name="hyena_mix",
    )(u3, u3, u3, conv_w, conv_w, conv_w, cb, cb, cb, hr, hi, hr, hi, hn, hn, skip,
      *mats["fe"], *mats["fo"], *mats["fe_t"], *mats["fo_t"])


def _sink_softmax_pv(scores, values, sink):
    m = sink
    for s in scores:
        m = jnp.maximum(m, jnp.max(s, axis=-1, keepdims=True))
    ps = [jnp.exp(s - m) for s in scores]
    den = jnp.exp(sink - m)
    for p in ps:
        den = den + jnp.sum(p, axis=-1, keepdims=True)
    inv = 1.0 / den
    out = None
    for p, v in zip(ps, values):
        t = _bdot((p * inv).astype(BF16), v)
        out = t if out is None else out + t
    return out


def _ctx_attn_kernel(sink_ref, q_ref, k_ref, v_ref, o_ref):
    h = pl.program_id(1)
    k = k_ref[0].astype(BF16)
    v = v_ref[0].astype(BF16)
    for g in range(GQA):
        cols = slice(g * HEAD_DIM, (g + 1) * HEAD_DIM)
        q = q_ref[0, :, cols].astype(BF16)
        s = lax.dot_general(q, k, (((1,), (1,)), ((), ())), preferred_element_type=F32) * SCALE
        o = _sink_softmax_pv([s], [v], sink_ref[h * GQA + g])
        o_ref[0, :, cols] = o.astype(BF16)


def _ctx_attn_call(u3, sink):
    B, S, _ = u3.shape
    gw = GQA * HEAD_DIM
    return pl.pallas_call(
        _ctx_attn_kernel,
        grid=(B, N_KV_HEADS),
        in_specs=[
            pl.BlockSpec(memory_space=pltpu.SMEM),
            pl.BlockSpec((1, S, gw), lambda b, h: (b, 0, Q_OFF // gw + h)),
            pl.BlockSpec((1, S, HEAD_DIM), lambda b, h: (b, 0, K_OFF // HEAD_DIM + h)),
            pl.BlockSpec((1, S, HEAD_DIM), lambda b, h: (b, 0, V_OFF // HEAD_DIM + h)),
        ],
        out_specs=pl.BlockSpec((1, S, gw), lambda b, h: (b, 0, h)),
        out_shape=jax.ShapeDtypeStruct((B, S, ATT_WIDTH), BF16),
        compiler_params=_cparams(("arbitrary", "arbitrary")),
        name="context_attention",
    )(sink, u3, u3, u3)


def _rope(x, cos, sin_signed):
    half = ROPE_AXIS_DIM // 2
    width = x.shape[1]
    lane = lax.broadcasted_iota(jnp.int32, x.shape, 1)
    low_half = (lane & (ROPE_AXIS_DIM - 1)) < half
    partner = jnp.where(low_half, pltpu.roll(x, width - half, axis=1), pltpu.roll(x, half, axis=1))
    return x * cos + partner * sin_signed


def _lat_attn_kernel(sink_ref, q_ref, k_ref, v_ref, ck_ref, cv_ref, cos_ref, sin_ref, o_ref, kr_ref, vb_ref):
    h = pl.program_id(1)
    L = k_ref.shape[1]
    nb = L // BLOCK
    rows4 = GQA * BLOCK
    c2 = SCALE * math.log2(math.e)
    kr_ref[...] = _rope(k_ref[0], cos_ref[:, :HEAD_DIM], sin_ref[:, :HEAD_DIM]).astype(BF16)
    vb_ref[...] = v_ref[0].astype(BF16)
    ck = ck_ref[0, 0].astype(BF16)
    cv = cv_ref[0, 0].astype(BF16)
    nt = (((1,), (1,)), ((), ()))

    r = lax.broadcasted_iota(jnp.int32, (rows4, 1), 0)
    sink = jnp.full((rows4, 1), sink_ref[h * GQA], F32)
    for g in range(1, GQA):
        sink = jnp.where(r >= g * BLOCK, sink_ref[h * GQA + g], sink)
    sink = sink * (1.0 / SCALE)
    a = r & (BLOCK - 1)
    b = lax.broadcasted_iota(jnp.int32, (1, BLOCK), 1)
    keep_prev = b >= a
    keep_next = b <= a

    for n in range(nb):
        rows = slice(n * BLOCK, (n + 1) * BLOCK)
        qa = _rope(q_ref[0, rows, :], cos_ref[rows, :], sin_ref[rows, :]).astype(BF16)
        q = jnp.concatenate([qa[:, g * HEAD_DIM:(g + 1) * HEAD_DIM] for g in range(GQA)], axis=0)
        scores, values = [], []
        for kb, keep in ((n - 1, keep_prev), (n, None), (n + 1, keep_next)):
            if 0 <= kb < nb:
                krows = slice(kb * BLOCK, (kb + 1) * BLOCK)
                s = lax.dot_general(q, kr_ref[krows, :], nt, preferred_element_type=F32)
                scores.append(s if keep is None else jnp.where(keep, s, -1e30))
                values.append(vb_ref[krows, :])
        scores.append(lax.dot_general(q, ck, nt, preferred_element_type=F32))
        values.append(cv)
        m = sink
        for s in scores:
            m = jnp.maximum(m, jnp.max(s, axis=-1, keepdims=True))
        ps = [jnp.exp2((s - m) * c2) for s in scores]
        den = jnp.exp2((sink - m) * c2)
        for p in ps:
            den = den + jnp.sum(p, axis=-1, keepdims=True)
        inv = 1.0 / den
        o = None
        for p, v in zip(ps, values):
            t = _bdot((p * inv).astype(BF16), v)
            o = t if o is None else o + t
        for g in range(GQA):
            o_ref[0, rows, g * HEAD_DIM:(g + 1) * HEAD_DIM] = o[g * BLOCK:(g + 1) * BLOCK].astype(BF16)


def _lat_attn_call(u3, cache_k_l, cache_v_l, layer, sink, cos, sin_signed):
    B, L, _ = u3.shape
    P = cache_k_l.shape[2]
    gw = GQA * HEAD_DIM
    c_spec = pl.BlockSpec((1, 1, P, HEAD_DIM), lambda b, h: (b, layer, 0, h))
    return pl.pallas_call(
        _lat_attn_kernel,
        grid=(B, N_KV_HEADS),
        in_specs=[
            pl.BlockSpec(memory_space=pltpu.SMEM),
            pl.BlockSpec((1, L, gw), lambda b, h: (b, 0, Q_OFF // gw + h)),
            pl.BlockSpec((1, L, HEAD_DIM), lambda b, h: (b, 0, K_OFF // HEAD_DIM + h)),
            pl.BlockSpec((1, L, HEAD_DIM), lambda b, h: (b, 0, V_OFF // HEAD_DIM + h)),
            c_spec, c_spec,
            _const_spec((L, gw)), _const_spec((L, gw)),
        ],
        out_specs=pl.BlockSpec((1, L, gw), lambda b, h: (b, 0, h)),
        out_shape=jax.ShapeDtypeStruct((B, L, ATT_WIDTH), BF16),
        scratch_shapes=[pltpu.VMEM((L, HEAD_DIM), BF16), pltpu.VMEM((L, HEAD_DIM), BF16)],
        compiler_params=_cparams(("arbitrary", "arbitrary")),
        name="latent_attention",
    )(sink, u3, u3, u3, cache_k_l, cache_v_l, jnp.tile(cos, (1, GQA)), jnp.tile(sin_signed, (1, GQA)))


def _rope_tables(L):
    rows = L // GRID_W
    row_pos = jnp.repeat(jnp.arange(rows), GRID_W)
    col_pos = jnp.arange(L) % GRID_W
    half = ROPE_AXIS_DIM // 2
    inv = ROPE_THETA ** (-jnp.arange(half, dtype=F32) * 2.0 / ROPE_AXIS_DIM)
    cos_parts, sin_parts = [], []
    for pos in (row_pos, col_pos):
        ang = pos.astype(F32)[:, None] * inv[None, :]
        cos_parts += [jnp.cos(ang), jnp.cos(ang)]
        sin_parts += [-jnp.sin(ang), jnp.sin(ang)]
    return jnp.concatenate(cos_parts, axis=-1), jnp.concatenate(sin_parts, axis=-1)


def _outproj_kernel(f_ref, hy_ref, att_ref, wf_ref, wh_ref, wa_ref, o_ref):
    o_ref[...] = (_bdot(f_ref[...], wf_ref[...].astype(BF16)) + _bdot(hy_ref[...], wh_ref[...].astype(BF16))
                  + _bdot(att_ref[...], wa_ref[...].astype(BF16)))


def _outproj_call(f, hy, att, w_out, layer):
    T = f.shape[0]
    assert F_WIDTH == HY_WIDTH and ATT_WIDTH == F_WIDTH + HY_WIDTH
    return pl.pallas_call(
        _outproj_kernel,
        grid=(T // TM_MM, D_MODEL // TN_MM),
        in_specs=[
            pl.BlockSpec((TM_MM, F_WIDTH), lambda i, j: (i, 0)),
            pl.BlockSpec((TM_MM, HY_WIDTH), lambda i, j: (i, 0)),
            pl.BlockSpec((TM_MM, ATT_WIDTH), lambda i, j: (i, 0)),
            pl.BlockSpec((None, F_WIDTH, TN_MM), lambda i, j: (layer, 0, j)),
            pl.BlockSpec((None, HY_WIDTH, TN_MM), lambda i, j: (layer, 1, j)),
            pl.BlockSpec((None, ATT_WIDTH, TN_MM), lambda i, j: (layer, 1, j)),
        ],
        out_specs=pl.BlockSpec((TM_MM, TN_MM), lambda i, j: (i, j)),
        out_shape=jax.ShapeDtypeStruct((T, D_MODEL), F32),
        compiler_params=_cparams(("arbitrary", "arbitrary")),
        name="outproj",
    )(f, hy, att, w_out, w_out, w_out)


def _ffn_kernel(seq_len, nf, h_ref, wg_ref, wv_ref, cwg_ref, cwv_ref, cbg_ref, cbv_ref, wd_ref, o_ref,
                u_ref, act_ref):
    g = pl.program_id(0)
    tm = h_ref.shape[0]
    seg = min(seq_len, tm)
    n_seg = tm // seg
    base = [SUBLANE + s * (seg + SUBLANE) for s in range(n_seg)]

    @pl.when(g == 0)
    def _():
        u_ref[...] = jnp.zeros(u_ref.shape, F32)
        act_ref[...] = jnp.zeros(act_ref.shape, BF16)

    @pl.when((g == 0) | (lax.rem(g + (nf - 2), nf) == 0))
    def _():
        o_ref[...] = jnp.zeros(o_ref.shape, F32)

    live = (g > 0).astype(F32)

    def conv_piece(slot, p):
        rp = tm // FFN_CONV_PIECES
        s, r0 = divmod(p * rp, seg)
        lo = base[s] + r0

        def conv(part, cw_ref, cb_ref):
            win = lambda off: u_ref[1 - slot, part, lo + off:lo + off + rp, :]
            return win(-1) * cw_ref[0:1, :] + win(0) * cw_ref[1:2, :] + win(1) * cw_ref[2:3, :] + cb_ref[...]

        gate = conv(0, cwg_ref, cbg_ref)
        val = conv(1, cwv_ref, cbv_ref)
        act_ref[1 - slot, p * rp:(p + 1) * rp, :] = (_silu(gate) * (val * live)).astype(BF16)

    def down_cols(slot, p):
        cp = D_MODEL // FFN_PIECES
        cols = slice(p * cp, (p + 1) * cp)
        o_ref[:, cols] += _bdot(act_ref[slot], wd_ref[:, cols].astype(BF16))

    def up_rows(slot, part, r):
        w_ref = (wg_ref, wv_ref)[part]
        s, r0 = divmod(r * up_rb, seg)
        u_ref[slot, part, base[s] + r0:base[s] + r0 + up_rb, :] = _bdot(
            h_ref[r * up_rb:(r + 1) * up_rb, :], w_ref[...].astype(BF16))

    up_rb = min(seg, FFN_UP_ROWS)
    segments = [(functools.partial(down_cols, p=p), FFN_DOWN_SHARE * tm * (D_MODEL // FFN_PIECES))
                for p in range(FFN_PIECES)]
    segments += [(functools.partial(up_rows, part=part, r=r), up_rb * D_MODEL)
                 for part in (0, 1) for r in range(tm // up_rb)]
    total_time = sum(t for _, t in segments[:-1])

    def step(slot):
        done, emitted = 0, 0
        for run, t in segments:
            while emitted < FFN_CONV_PIECES and emitted * total_time <= done * FFN_CONV_PIECES:
                conv_piece(slot, emitted)
                emitted += 1
            run(slot)
            done += t
        for p in range(emitted, FFN_CONV_PIECES):
            conv_piece(slot, p)

    for slot in (0, 1):
        pl.when((g & 1) == slot)(functools.partial(step, slot))


def _ffn_call(h, seq_len, w_up, w_up_val, conv_w, conv_b, w_down, layer):
    T = h.shape[0]
    nf = D_FF // TF_FFN
    n_tiles = T // TM_FFN
    assert TM_FFN % seq_len == 0
    cb = conv_b.reshape(DEPTH, 1, -1)
    wspec = lambda shape, imap: pl.BlockSpec((None,) + shape, imap)
    seg = min(seq_len, TM_FFN)
    u_rows = SUBLANE + (TM_FFN // seg) * (seg + SUBLANE)
    up_chunk = lambda g: lax.rem(g, nf)
    cv_chunk = lambda g: lax.rem(g + (nf - 1), nf)
    dn_chunk = lambda g: lax.rem(g + (nf - 2), nf)
    up_tile = lambda g: jnp.minimum(g // nf, n_tiles - 1)
    dn_tile = lambda g: jnp.clip((g + (nf - 2)) // nf - 1, 0, n_tiles - 1)
    val_off = nf if w_up_val.shape[-1] == 2 * D_FF else 0
    return pl.pallas_call(
        functools.partial(_ffn_kernel, seq_len, nf),
        grid=(n_tiles * nf + 2,),
        in_specs=[
            pl.BlockSpec((TM_FFN, D_MODEL), lambda g: (up_tile(g), 0), pipeline_mode=pl.Buffered(1)),
            wspec((D_MODEL, TF_FFN), lambda g: (layer, 0, up_chunk(g))),
            wspec((D_MODEL, TF_FFN), lambda g: (layer, 0, val_off + up_chunk(g))),
            wspec((3, TF_FFN), lambda g: (layer, 0, cv_chunk(g))),
            wspec((3, TF_FFN), lambda g: (layer, 0, nf + cv_chunk(g))),
            wspec((1, TF_FFN), lambda g: (layer, 0, cv_chunk(g))),
            wspec((1, TF_FFN), lambda g: (layer, 0, nf + cv_chunk(g))),
            wspec((TF_FFN, D_MODEL), lambda g: (layer, dn_chunk(g), 0)),
        ],
        out_specs=pl.BlockSpec((TM_FFN, D_MODEL), lambda g: (dn_tile(g), 0), pipeline_mode=pl.Buffered(1)),
        out_shape=jax.ShapeDtypeStruct((T, D_MODEL), F32),
        scratch_shapes=[pltpu.VMEM((2, 2, u_rows, TF_FFN), F32), pltpu.VMEM((2, TM_FFN, TF_FFN), BF16)],
        compiler_params=_cparams(("arbitrary",)),
        name="conv_ffn",
    )(h, w_up, w_up_val, conv_w, conv_w, cb, cb, w_down)


def _layer(x, h, B, L, layer, mods, modrow_of_tile, big, p, attend):
    T = B * L
    u3 = _inproj_call(h, big["w_in"], layer).reshape(B, L, IN_WIDTH)
    f = _fourier_call(u3, L)
    spectrum = _filters_call(L, p["flt_w1"], p["flt_b1"], p["flt_w2"], p["flt_b2"], p["flt_w3"], p["flt_freq"])
    hy = _hyena_call(u3, L, p["hy_conv_w"], p["hy_conv_b"], spectrum, p["hy_skip"])
    att = attend(u3)
    proj = _outproj_call(f.reshape(T, F_WIDTH), hy.reshape(T, HY_WIDTH), att.reshape(T, ATT_WIDTH),
                         big["w_out"], layer)
    x, h2 = _resid_ln_call(x, proj, mods, layer, 2, big["ln1_g"], big["ln1_b"], modrow_of_tile,
                           h_layer=layer, h_row=3)
    ff = _ffn_call(h2, L, big["ffn_w_up"], big["ffn_w_up_val"], big["ffn_conv_w"], big["ffn_conv_b"],
                   big["ffn_w_down"], layer)
    if layer + 1 < DEPTH:
        x, h_next = _resid_ln_call(x, ff, mods, layer, 5, big["ln2_g"], big["ln2_b"], modrow_of_tile,
                                   h_layer=layer + 1, h_row=0)
    else:
        x, h_next = _resid_ln_call(x, ff, mods, layer, 5, big["ln2_g"], big["ln2_b"], modrow_of_tile)
    return x, h_next, u3


def kernel(x_prompt, x_sample, cache_k, cache_v, c, c_ctx, w_mod, b_mod, w_in, w_out, attn_sink,
           hy_conv_w, hy_conv_b, flt_w1, flt_b1, flt_w2, flt_b2, flt_w3, flt_freq, hy_skip,
           ln1_g, ln1_b, ffn_w_up, ffn_conv_w, ffn_conv_b, ffn_w_down, ln2_g, ln2_b):
    Bp, S, _ = x_prompt.shape
    Bs, Ls, _ = x_sample.shape
    past = cache_k.shape[2]

    cvec = jnp.zeros((MOD_ROWS, D_MODEL), F32).at[0].set(c_ctx).at[1:1 + Bs].set(c)
    mods = _mod_call(cvec, w_mod, b_mod).reshape(DEPTH, MOD_ROWS, 6, D_MODEL)

    big = dict(w_in=w_in, w_out=w_out, ffn_w_up=ffn_w_up,
               ffn_w_up_val=ffn_w_up,
               ffn_w_down=ffn_w_down, ffn_conv_w=ffn_conv_w, ffn_conv_b=ffn_conv_b,
               ln1_g=ln1_g, ln1_b=ln1_b, ln2_g=ln2_g, ln2_b=ln2_b)
    ck = cache_k.reshape(Bs, DEPTH, past, KV_WIDTH)
    cv = cache_v.reshape(Bs, DEPTH, past, KV_WIDTH)
    cos, sin_signed = _rope_tables(Ls)

    ctx_row = lambda i, tm: 0
    lat_row = lambda i, tm: 1 + i * tm // Ls

    xp = x_prompt.reshape(Bp * S, D_MODEL)
    xs = x_sample.reshape(Bs * Ls, D_MODEL)
    hp = _ln_mod_call(xp, mods, 0, ctx_row)
    hs = _ln_mod_call(xs, mods, 0, lat_row)
    ctx_k, ctx_v = [], []
    for l in range(DEPTH):
        p = dict(flt_w1=flt_w1[l], flt_b1=flt_b1[l], flt_w2=flt_w2[l], flt_b2=flt_b2[l], flt_w3=flt_w3[l],
                 flt_freq=flt_freq[l], hy_conv_w=hy_conv_w[l], hy_conv_b=hy_conv_b[l], hy_skip=hy_skip[l])
        sink = attn_sink[l]
        xp, hp, up3 = _layer(xp, hp, Bp, S, l, mods, ctx_row, big, p, lambda u3: _ctx_attn_call(u3, sink))
        ctx_k.append(up3[:, :, K_OFF:K_OFF + KV_WIDTH].reshape(Bp, S, N_KV_HEADS, HEAD_DIM))
        ctx_v.append(up3[:, :, V_OFF:V_OFF + KV_WIDTH].reshape(Bp, S, N_KV_HEADS, HEAD_DIM))
        xs, hs, _ = _layer(xs, hs, Bs, Ls, l, mods, lat_row, big, p,
                           lambda u3: _lat_attn_call(u3, ck, cv, l, sink, cos, sin_signed))

    return (xp.reshape(Bp, S, D_MODEL), xs.reshape(Bs, Ls, D_MODEL),
            jnp.stack(ctx_k, axis=1), jnp.stack(ctx_v, axis=1))
```

```python
import functools
import math

import ml_dtypes
import numpy as np
import jax
import jax.numpy as jnp
from jax import lax
from jax.experimental import pallas as pl
from jax.experimental.pallas import tpu as pltpu

F32 = jnp.float32
BF16 = jnp.bfloat16

D_MODEL = 4096
DEPTH = 2
GRID_W = 64
F_WIDTH = D_MODEL // 4
F_GROUPS = 8
F_GROUP_W = F_WIDTH // F_GROUPS
HY_WIDTH = D_MODEL // 4
HY_ORDER = 2
N_BANDS = 16
POS_EMB = 1 + 2 * N_BANDS
FILTER_HID = 64
DECAY_TARGET = 1e-2
MIN_DECAY = math.log(DECAY_TARGET) / 1.5
MAX_DECAY = math.log(DECAY_TARGET) / 0.3
HEAD_DIM = 128
N_Q_HEADS = (D_MODEL // 2) // HEAD_DIM
N_KV_HEADS = N_Q_HEADS // 4
GQA = N_Q_HEADS // N_KV_HEADS
ATT_WIDTH = N_Q_HEADS * HEAD_DIM
KV_WIDTH = N_KV_HEADS * HEAD_DIM
WINDOW = 128
BLOCK = 128
SCALE = HEAD_DIM ** -0.5
ROPE_THETA = 10000.0
ROPE_AXIS_DIM = HEAD_DIM // 2
D_FF = 256 * ((8 * D_MODEL // 3 + 255) // 256)
LN_EPS = 1e-5
ALPHA = (2 * DEPTH) ** 0.25
IN_WIDTH = F_WIDTH + 3 * HY_WIDTH + ATT_WIDTH + 2 * KV_WIDTH
MIX_WIDTH = F_WIDTH + HY_WIDTH + ATT_WIDTH

HY_OFF = F_WIDTH
Q_OFF = F_WIDTH + 3 * HY_WIDTH
K_OFF = Q_OFF + ATT_WIDTH
V_OFF = K_OFF + KV_WIDTH

MOD_ROWS = 16
LANE = 128
SUBLANE = 8
VMEM_LIMIT = 56 * 1024 * 1024

BF16_ROWS = 16
TM_MM = 1024
TN_MM = 512
TM_LN = 256
TM_FFN = 1024
TF_FFN = 256
FFN_UP_ROWS = 256
FFN_PIECES = 16
FFN_DOWN_SHARE = 2
FFN_CONV_PIECES = 16
HY_CT = 256
TN_MOD = 512
TK_MOD = 512
ROW_CHUNK = 32


def _cparams(sem):
    return pltpu.CompilerParams(dimension_semantics=sem, vmem_limit_bytes=VMEM_LIMIT)


def _const_spec(shape):
    return pl.BlockSpec(shape, lambda *_: (0,) * len(shape), pipeline_mode=pl.Buffered(1))


def _bdot(a, b):
    return jnp.dot(a, b, preferred_element_type=F32)


def _ln_rows(x):
    mu = jnp.mean(x, axis=-1, keepdims=True)
    xc = x - mu
    var = jnp.mean(xc * xc, axis=-1, keepdims=True)
    return xc * lax.rsqrt(var + LN_EPS)


def _silu(x):
    return x / (1.0 + jnp.exp(-x))


def _for_row_chunks(n_rows, body):
    def step(c, carry):
        body(pl.ds(pl.multiple_of(c * ROW_CHUNK, ROW_CHUNK), ROW_CHUNK))
        return carry
    lax.fori_loop(0, n_rows // ROW_CHUNK, step, 0)


def _hi_lo(x):
    hi = x.astype(ml_dtypes.bfloat16)
    lo = (x - hi.astype(np.float64)).astype(ml_dtypes.bfloat16)
    return jnp.asarray(hi), jnp.asarray(lo)


@functools.lru_cache(maxsize=None)
def _dft_mats(L):
    H = L // 2
    k = np.arange(H)
    t = np.arange(H)
    ang_e = 2.0 * np.pi * ((k[:, None] * t[None, :]) % L) / L
    fe = np.concatenate([np.cos(ang_e), -np.sin(ang_e)], axis=0)
    ang_o = np.pi * ((k[:, None] * (2 * t[None, :] + 1)) % (2 * L)) / L
    fo = np.concatenate([np.cos(ang_o), -np.sin(ang_o)], axis=0)
    bins = np.concatenate([k, L - k])
    n = np.arange(L)
    ang_s = np.pi * ((bins[:, None] * n[None, :]) % (2 * L)) / L
    wgt = np.where(np.concatenate([k, k]) == 0, 1.0, 2.0)[:, None] / (2 * L)
    fsel = np.concatenate([np.cos(ang_s) * wgt, -np.sin(ang_s) * wgt], axis=0)
    ang_me = 2.0 * np.pi * ((k[:, None] * (2 * t[None, :])) % L) / L
    ang_mo = 2.0 * np.pi * ((k[:, None] * (2 * t[None, :] + 1)) % L) / L
    fme = np.concatenate([np.cos(ang_me), -np.sin(ang_me)], axis=1) / np.sqrt(L)
    fmo = np.concatenate([np.cos(ang_mo), -np.sin(ang_mo)], axis=1) / np.sqrt(L)
    return dict(fe=_hi_lo(fe), fo=_hi_lo(fo), fe_t=_hi_lo(fe.T), fo_t=_hi_lo(fo.T), fsel=_hi_lo(fsel),
                fme=_hi_lo(fme), fmo=_hi_lo(fmo))


@functools.lru_cache(maxsize=None)
def _chan_dft():
    n = np.arange(F_GROUP_W)
    ang = 2.0 * np.pi * ((n[:, None] * n[None, :]) % F_GROUP_W) / F_GROUP_W
    return _hi_lo(np.concatenate([np.cos(ang), np.sin(ang)], axis=1) / np.sqrt(F_GROUP_W))


def _mod_kernel(c_ref, w_ref, b_ref, o_ref):
    acc = b_ref[0] + jnp.zeros((MOD_ROWS, TN_MOD), F32)
    for k in range(0, D_MODEL, TK_MOD):
        s = _silu(c_ref[:, k:k + TK_MOD]).astype(BF16)
        acc = acc + _bdot(s, w_ref[0, k:k + TK_MOD, :].astype(BF16))
    o_ref[0] = acc


def _mod_call(cvec, w_mod, b_mod):
    n_out = w_mod.shape[-1]
    return pl.pallas_call(
        _mod_kernel,
        grid=(DEPTH, n_out // TN_MOD),
        in_specs=[
            pl.BlockSpec((MOD_ROWS, D_MODEL), lambda l, j: (0, 0)),
            pl.BlockSpec((1, D_MODEL, TN_MOD), lambda l, j: (l, 0, j)),
            pl.BlockSpec((1, 1, TN_MOD), lambda l, j: (l, 0, j)),
        ],
        out_specs=pl.BlockSpec((1, MOD_ROWS, TN_MOD), lambda l, j: (l, 0, j)),
        out_shape=jax.ShapeDtypeStruct((DEPTH, MOD_ROWS, n_out), F32),
        compiler_params=_cparams(("arbitrary", "arbitrary")),
        name="adaln_mod",
    )(cvec, w_mod, b_mod.reshape(DEPTH, 1, n_out))


def _inproj_kernel(h_ref, w_ref, o_ref):
    o_ref[...] = _bdot(h_ref[...], w_ref[...].astype(BF16))


def _inproj_call(h, w_in, layer):
    T = h.shape[0]
    return pl.pallas_call(
        _inproj_kernel,
        grid=(T // TM_MM, IN_WIDTH // TN_MM),
        in_specs=[
            pl.BlockSpec((TM_MM, D_MODEL), lambda i, j: (i, 0)),
            pl.BlockSpec((None, D_MODEL, TN_MM), lambda i, j: (layer, 0, j)),
        ],
        out_specs=pl.BlockSpec((TM_MM, TN_MM), lambda i, j: (i, j)),
        out_shape=jax.ShapeDtypeStruct((T, IN_WIDTH), F32),
        compiler_params=_cparams(("arbitrary", "arbitrary")),
        name="inproj",
    )(h, w_in)


def _mod_spec(layer, modrow_of_tile, tm):
    return pl.BlockSpec((None, None, 6, D_MODEL), lambda i: (layer, modrow_of_tile(i, tm), 0, 0))


def _ln_mod_kernel(x_ref, mod_ref, h_ref):
    def body(rows):
        h = _ln_rows(x_ref[rows, :]) * (1.0 + mod_ref[1:2, :]) + mod_ref[0:1, :]
        h_ref[rows, :] = h.astype(BF16)
    _for_row_chunks(TM_LN, body)


def _ln_mod_call(x, mods, layer, modrow_of_tile):
    T = x.shape[0]
    return pl.pallas_call(
        _ln_mod_kernel,
        grid=(T // TM_LN,),
        in_specs=[pl.BlockSpec((TM_LN, D_MODEL), lambda i: (i, 0)), _mod_spec(layer, modrow_of_tile, TM_LN)],
        out_specs=pl.BlockSpec((TM_LN, D_MODEL), lambda i: (i, 0)),
        out_shape=jax.ShapeDtypeStruct((T, D_MODEL), BF16),
        compiler_params=_cparams(("arbitrary",)),
        name="ln_modulate",
    )(x, mods)


def _resid_ln_kernel(gate_row, h_row, x_ref, d_ref, modg_ref, g_ref, b_ref, *rest):
    if h_row is None:
        (xo_ref,) = rest
    else:
        modh_ref, xo_ref, ho_ref = rest

    def body(rows):
        y = ALPHA * x_ref[rows, :] + modg_ref[gate_row:gate_row + 1, :] * d_ref[rows, :]
        xn = _ln_rows(y) * g_ref[...] + b_ref[...]
        xo_ref[rows, :] = xn
        if h_row is not None:
            h = _ln_rows(xn) * (1.0 + modh_ref[h_row + 1:h_row + 2, :]) + modh_ref[h_row:h_row + 1, :]
            ho_ref[rows, :] = h.astype(BF16)
    _for_row_chunks(TM_LN, body)


def _resid_ln_call(x, d, mods, layer, gate_row, ln_g, ln_b, modrow_of_tile, h_layer=None, h_row=None):
    T = x.shape[0]
    tile = pl.BlockSpec((TM_LN, D_MODEL), lambda i: (i, 0))
    vec = pl.BlockSpec((None, 1, D_MODEL), lambda i: (layer, 0, 0))
    in_specs = [tile, tile, _mod_spec(layer, modrow_of_tile, TM_LN), vec, vec]
    args = [x, d, mods, ln_g.reshape(DEPTH, 1, D_MODEL), ln_b.reshape(DEPTH, 1, D_MODEL)]
    out_specs = [tile]
    out_shape = [jax.ShapeDtypeStruct((T, D_MODEL), F32)]
    if h_row is not None:
        in_specs.append(_mod_spec(h_layer, modrow_of_tile, TM_LN))
        args.append(mods)
        out_specs.append(tile)
        out_shape.append(jax.ShapeDtypeStruct((T, D_MODEL), BF16))
    out = pl.pallas_call(
        functools.partial(_resid_ln_kernel, gate_row, h_row),
        grid=(T // TM_LN,),
        in_specs=in_specs,
        out_specs=out_specs,
        out_shape=out_shape,
        compiler_params=_cparams(("arbitrary",)),
        name="residual_ln",
    )(*args)
    return out if h_row is not None else (out[0], None)


def _fourier_kernel(u_ref, cc_hi_ref, cc_lo_ref, fme_hi_ref, fme_lo_ref, fmo_hi_ref, fmo_lo_ref, o_ref,
                    pe_ref, po_ref, rows_ref):
    L = u_ref.shape[1]
    H = L // 2
    rc = min(H, 256)
    for g in range(F_GROUPS):
        cols = slice(g * F_GROUP_W, (g + 1) * F_GROUP_W)
        rows_ref[...] = u_ref[0, :, cols]
        for parity, p_ref in enumerate((pe_ref, po_ref)):
            ug = rows_ref[pl.ds(parity, H, stride=2), :].astype(BF16)
            a = _bdot(ug, cc_hi_ref[...]) + _bdot(ug, cc_lo_ref[...])
            p_ref[0:H, cols] = a[:, :F_GROUP_W].astype(BF16)
            p_ref[H:L, cols] = a[:, F_GROUP_W:].astype(BF16)
    for r in range(0, H, rc):
        rows = slice(r, r + rc)
        e = _bdot(fme_hi_ref[rows, :], pe_ref[...]) + _bdot(fme_lo_ref[rows, :], pe_ref[...])
        o = _bdot(fmo_hi_ref[rows, :], po_ref[...]) + _bdot(fmo_lo_ref[rows, :], po_ref[...])
        o_ref[0, r:r + rc, :] = (e + o).astype(BF16)
        o_ref[0, H + r:H + r + rc, :] = (e - o).astype(BF16)


def _fourier_call(u3, L):
    B = u3.shape[0]
    H = L // 2
    mats = _dft_mats(L)
    cc_hi, cc_lo = _chan_dft()
    return pl.pallas_call(
        _fourier_kernel,
        grid=(B,),
        in_specs=[
            pl.BlockSpec((1, L, F_WIDTH), lambda b: (b, 0, 0)),
            _const_spec((F_GROUP_W, 2 * F_GROUP_W)),
            _const_spec((F_GROUP_W, 2 * F_GROUP_W)),
            _const_spec((H, L)), _const_spec((H, L)), _const_spec((H, L)), _const_spec((H, L)),
        ],
        out_specs=pl.BlockSpec((1, L, F_WIDTH), lambda b: (b, 0, 0)),
        out_shape=jax.ShapeDtypeStruct((B, L, F_WIDTH), BF16),
        scratch_shapes=[pltpu.VMEM((L, F_WIDTH), BF16), pltpu.VMEM((L, F_WIDTH), BF16),
                        pltpu.VMEM((L, F_GROUP_W), F32)],
        compiler_params=_cparams(("arbitrary",)),
        name="fourier_mix",
    )(u3, cc_hi, cc_lo, *mats["fme"], *mats["fmo"])


def _filter_taps_kernel(feat_ref, w1_ref, b1_ref, w2_ref, b2_ref, freq_ref, w3f_ref, w3b_ref, decay_ref,
                        fwd_ref, bsh_ref, h_ref):
    hp = lax.Precision.HIGHEST
    feats = feat_ref[...]

    @pl.when((pl.program_id(0) == 0) & (pl.program_id(1) == 0))
    def _():
        freq = freq_ref[...]
        h1 = jnp.sin(freq * (jnp.dot(feats, w1_ref[...], precision=hp, preferred_element_type=F32) + b1_ref[...]))
        h_ref[...] = jnp.sin(freq * (jnp.dot(h1, w2_ref[...], precision=hp, preferred_element_type=F32)
                                     + b2_ref[...]))

    h = h_ref[...]
    win = jnp.exp(-feats[:, 0:1] * decay_ref[...])
    fwd = jnp.dot(h, w3f_ref[...], precision=hp, preferred_element_type=F32) * win
    bwd = jnp.dot(h, w3b_ref[...], precision=hp, preferred_element_type=F32) * win
    row = lax.broadcasted_iota(jnp.int32, bwd.shape, 0)
    bsh = jnp.where(row == 0, 0.0, pltpu.roll(bwd, 1, axis=0))
    norm = jnp.sum(jnp.abs(fwd), axis=0, keepdims=True) + jnp.sum(jnp.abs(bsh), axis=0, keepdims=True)
    fwd_ref[...] = fwd / norm
    bsh_ref[...] = bsh / norm


def _filter_spectrum_kernel(fwd_ref, bsh_ref, f_hi_ref, f_lo_ref, gr_ref, gi_ref, gh_ref):
    L = fwd_ref.shape[0]
    fwd = fwd_ref[...]
    bsh = bsh_ref[...]
    both = jnp.concatenate([fwd, bsh], axis=1)
    b_hi = both.astype(BF16)
    b_lo = (both - b_hi.astype(F32)).astype(BF16)
    ct = fwd.shape[1]
    rc = min(L, 256)
    for r in range(0, L, rc):
        def spec(rows):
            return (_bdot(f_hi_ref[rows, :], b_hi) + _bdot(f_lo_ref[rows, :], b_hi)
                    + _bdot(f_hi_ref[rows, :], b_lo))
        c = spec(slice(r, r + rc))
        s = spec(slice(L + r, L + r + rc))
        gr_ref[r:r + rc, :] = c[:, :ct] + c[:, ct:]
        gi_ref[r:r + rc, :] = s[:, :ct] - s[:, ct:]
    q = lax.broadcasted_iota(jnp.int32, fwd.shape, 0) & 3
    c4 = jnp.where(q == 0, 1.0, jnp.where(q == 2, -1.0, 0.0))
    s4 = jnp.where(q == 1, 1.0, jnp.where(q == 3, -1.0, 0.0))
    gh_ref[0:1, :] = jnp.sum(c4 * (fwd + bsh), axis=0, keepdims=True) * (1.0 / L)
    gh_ref[1:2, :] = jnp.sum(s4 * (bsh - fwd), axis=0, keepdims=True) * (1.0 / L)


def _filters_call(L, w1, b1, w2, b2, w3, freq):
    t = np.arange(L, dtype=np.float32)
    t_norm = t / np.float32(max(L - 1, 1))
    w = np.float32(2.0 * math.pi) * t / np.float32(L)
    bands = np.linspace(1e-4, N_BANDS - 1, N_BANDS, dtype=np.float32)
    feats = np.zeros((L, LANE), np.float32)
    feats[:, 0] = t_norm
    feats[:, 1:1 + N_BANDS] = np.cos(w[:, None] * bands[None])
    feats[:, 1 + N_BANDS:POS_EMB] = np.sin(w[:, None] * bands[None])
    decay = np.abs(np.linspace(MIN_DECAY, MAX_DECAY, HY_WIDTH, dtype=np.float32))[None, :]
    w1p = jnp.zeros((LANE, FILTER_HID), F32).at[:POS_EMB].set(w1)
    nct = HY_WIDTH // HY_CT
    ocw = HY_ORDER * HY_WIDTH
    row = lambda v: v.reshape(1, -1)
    small = lambda shape: pl.BlockSpec(shape, lambda o, ct: (0, 0))
    fwd, bsh = pl.pallas_call(
        _filter_taps_kernel,
        grid=(HY_ORDER, nct),
        in_specs=[
            small((L, LANE)), small((LANE, FILTER_HID)), small((1, FILTER_HID)),
            small((FILTER_HID, FILTER_HID)), small((1, FILTER_HID)), small((1, FILTER_HID)),
            pl.BlockSpec((FILTER_HID, HY_CT), lambda o, ct: (0, o * 2 * nct + ct)),
            pl.BlockSpec((FILTER_HID, HY_CT), lambda o, ct: (0, o * 2 * nct + nct + ct)),
            pl.BlockSpec((1, HY_CT), lambda o, ct: (0, ct)),
        ],
        out_specs=[pl.BlockSpec((L, HY_CT), lambda o, ct: (0, o * nct + ct))] * 2,
        out_shape=[jax.ShapeDtypeStruct((L, ocw), F32)] * 2,
        scratch_shapes=[pltpu.VMEM((L, FILTER_HID), F32)],
        compiler_params=_cparams(("arbitrary", "arbitrary")),
        name="hyena_filter_taps",
    )(jnp.asarray(feats), w1p, row(b1), w2, row(b2), row(freq), w3, w3, jnp.asarray(decay))
    f_hi, f_lo = _dft_mats(L)["fsel"]
    return pl.pallas_call(
        _filter_spectrum_kernel,
        grid=(ocw // HY_CT,),
        in_specs=[
            pl.BlockSpec((L, HY_CT), lambda j: (0, j)),
            pl.BlockSpec((L, HY_CT), lambda j: (0, j)),
            _const_spec((2 * L, L)),
            _const_spec((2 * L, L)),
        ],
        out_specs=[pl.BlockSpec((L, HY_CT), lambda j: (0, j)),
                   pl.BlockSpec((L, HY_CT), lambda j: (0, j)),
                   pl.BlockSpec((2, HY_CT), lambda j: (0, j))],
        out_shape=[jax.ShapeDtypeStruct((L, ocw), F32), jax.ShapeDtypeStruct((L, ocw), F32),
                   jax.ShapeDtypeStruct((2, ocw), F32)],
        compiler_params=_cparams(("arbitrary",)),
        name="hyena_filter_spectrum",
    )(fwd, bsh, f_hi, f_lo)


def _hyena_kernel(v_ref, x1_ref, x2_ref, wv_ref, w1_ref, w2_ref, bv_ref, b1_ref, b2_ref,
                  gr0_ref, gi0_ref, gr1_ref, gi1_ref, gh0_ref, gh1_ref, skip_ref,
                  fe_hi_ref, fe_lo_ref, fo_hi_ref, fo_lo_ref, ie_hi_ref, ie_lo_ref, io_hi_ref, io_lo_ref,
                  o_ref, spec_e_ref, spec_o_ref, out_ref):
    L = v_ref.shape[1]
    H = L // 2
    rc = min(H, 256)
    row = lax.broadcasted_iota(jnp.int32, (H, HY_CT), 0)
    first = row == 0
    last = row == H - 1
    alt = jnp.where((row & 1) == 0, 1.0, -1.0)

    n_half = HY_CT // LANE

    def conv3(x_ref, w_ref, b_ref):
        for c in range(n_half):
            out_ref[c] = x_ref[0, :, c * LANE:(c + 1) * LANE]
        xe = jnp.concatenate([out_ref[c, pl.ds(0, H, stride=2), :] for c in range(n_half)], axis=1)
        xo = jnp.concatenate([out_ref[c, pl.ds(1, H, stride=2), :] for c in range(n_half)], axis=1)
        xo_prev = jnp.where(first, 0.0, pltpu.roll(xo, 1, axis=0))
        xe_next = jnp.where(last, 0.0, pltpu.roll(xe, H - 1, axis=0))
        w0, w1, w2, b = w_ref[0:1, :], w_ref[1:2, :], w_ref[2:3, :], b_ref[...]
        return xo_prev * w0 + xe * w1 + xo * w2 + b, xe * w0 + xo * w1 + xe_next * w2 + b

    ze, zo = conv3(v_ref, wv_ref, bv_ref)
    gates = (conv3(x1_ref, w1_ref, b1_ref), conv3(x2_ref, w2_ref, b2_ref))
    filt = ((gr0_ref, gi0_ref, gh0_ref), (gr1_ref, gi1_ref, gh1_ref))
    two_pass = lambda hi_ref, lo_ref, rows, x: _bdot(hi_ref[rows, :], x) + _bdot(lo_ref[rows, :], x)
    for o in range(HY_ORDER):
        gr_ref, gi_ref, gh_ref = filt[o]
        zeb = ze.astype(BF16)
        zob = zo.astype(BF16)
        for r in range(0, H, rc):
            re_rows, im_rows = slice(r, r + rc), slice(H + r, H + r + rc)
            er = two_pass(fe_hi_ref, fe_lo_ref, re_rows, zeb)
            ei = two_pass(fe_hi_ref, fe_lo_ref, im_rows, zeb)
            orr = two_pass(fo_hi_ref, fo_lo_ref, re_rows, zob)
            oi = two_pass(fo_hi_ref, fo_lo_ref, im_rows, zob)
            zr, zi, zpr, zpi = er + orr, ei + oi, er - orr, oi - ei
            gr, gi, gpr, gpi = gr_ref[re_rows, :], gi_ref[re_rows, :], gr_ref[im_rows, :], gi_ref[im_rows, :]
            yr, yi = zr * gr - zi * gi, zr * gi + zi * gr
            ypr, ypi = zpr * gpr - zpi * gpi, zpr * gpi + zpi * gpr
            spec_e_ref[re_rows, :] = (yr + ypr).astype(BF16)
            spec_e_ref[im_rows, :] = (yi - ypi).astype(BF16)
            spec_o_ref[re_rows, :] = (yr - ypr).astype(BF16)
            spec_o_ref[im_rows, :] = (yi + ypi).astype(BF16)
        zhr = jnp.sum(alt * ze, axis=0, keepdims=True)
        zhi = -jnp.sum(alt * zo, axis=0, keepdims=True)
        ghr, ghi = gh_ref[0:1, :], gh_ref[1:2, :]
        yhr, yhi = zhr * ghr - zhi * ghi, zhr * ghi + zhi * ghr
        ye, yo = [], []
        for r in range(0, H, rc):
            rows = slice(r, r + rc)
            ye.append(two_pass(ie_hi_ref, ie_lo_ref, rows, spec_e_ref[...]))
            yo.append(two_pass(io_hi_ref, io_lo_ref, rows, spec_o_ref[...]))
        ye = jnp.concatenate(ye, axis=0) + alt * yhr
        yo = jnp.concatenate(yo, axis=0) - alt * yhi
        skip = skip_ref[o:o + 1, :]
        ze = gates[o][0] * (ye + skip * ze)
        zo = gates[o][1] * (yo + skip * zo)
    for c in range(n_half):
        lanes = slice(c * LANE, (c + 1) * LANE)
        out_ref[c, pl.ds(0, H, stride=2), :] = ze[:, lanes]
        out_ref[c, pl.ds(1, H, stride=2), :] = zo[:, lanes]
        o_ref[0, :, lanes] = out_ref[c].astype(BF16)


def _hyena_call(u3, L, conv_w, conv_b, spectrum, skip):
    B = u3.shape[0]
    H = L // 2
    nct = HY_WIDTH // HY_CT
    hr, hi, hn = spectrum
    mats = _dft_mats(L)
    off = HY_OFF // HY_CT
    u_spec = lambda part: pl.BlockSpec((1, L, HY_CT), lambda b, ct: (b, 0, off + part * nct + ct))
    cw_spec = lambda part: pl.BlockSpec((3, HY_CT), lambda b, ct: (0, part * nct + ct))
    cb_spec = lambda part: pl.BlockSpec((1, HY_CT), lambda b, ct: (0, part * nct + ct))
    h_spec = lambda o: pl.BlockSpec((L, HY_CT), lambda b, ct: (0, o * nct + ct))
    n_spec = lambda o: pl.BlockSpec((2, HY_CT), lambda b, ct: (0, o * nct + ct))
    cb = conv_b.reshape(1, -1)
    return pl.pallas_call(
        _hyena_kernel,
        grid=(B, nct),
        in_specs=[
            u_spec(0), u_spec(1), u_spec(2),
            cw_spec(0), cw_spec(1), cw_spec(2),
            cb_spec(0), cb_spec(1), cb_spec(2),
            h_spec(0), h_spec(0), h_spec(1), h_spec(1), n_spec(0), n_spec(1),
            pl.BlockSpec((HY_ORDER, HY_CT), lambda b, ct: (0, ct)),
            _const_spec((L, H)), _const_spec((L, H)), _const_spec((L, H)), _const_spec((L, H)),
            _const_spec((H, L)), _const_spec((H, L)), _const_spec((H, L)), _const_spec((H, L)),
        ],
        out_specs=pl.BlockSpec((1, L, HY_CT), lambda b, ct: (b, 0, ct)),
        out_shape=jax.ShapeDtypeStruct((B, L, HY_WIDTH), BF16),
        scratch_shapes=[pltpu.VMEM((L, HY_CT), BF16), pltpu.VMEM((L, HY_CT), BF16),
                        pltpu.VMEM((HY_CT // LANE, L, LANE), F32)],
        compiler_params=_cparams(("arbitrary", "arbitrary")),
        name="hyena_mix",
    )(u3, u3, u3, conv_w, conv_w, conv_w, cb, cb, cb, hr, hi, hr, hi, hn, hn, skip,
      *mats["fe"], *mats["fo"], *mats["fe_t"], *mats["fo_t"])


def _ctx_attn_kernel(sink_ref, q_ref, k_ref, v_ref, o_ref):
    h = pl.program_id(1)
    S = k_ref.shape[1]
    c2 = SCALE * math.log2(math.e)
    k = k_ref[0].astype(BF16)
    v = v_ref[0].astype(BF16)
    q = jnp.concatenate([q_ref[0, :, g * HEAD_DIM:(g + 1) * HEAD_DIM].astype(BF16) for g in range(GQA)], axis=0)
    r = lax.broadcasted_iota(jnp.int32, (GQA * S, 1), 0)
    sink = jnp.full((GQA * S, 1), sink_ref[h * GQA], F32)
    for g in range(1, GQA):
        sink = jnp.where(r >= g * S, sink_ref[h * GQA + g], sink)
    sink = sink * (1.0 / SCALE)
    s = lax.dot_general(q, k, (((1,), (1,)), ((), ())), preferred_element_type=F32)
    m = jnp.maximum(sink, jnp.max(s, axis=-1, keepdims=True))
    p = jnp.exp2((s - m) * c2)
    den = jnp.exp2((sink - m) * c2) + jnp.sum(p, axis=-1, keepdims=True)
    o = _bdot((p * (1.0 / den)).astype(BF16), v)
    for g in range(GQA):
        o_ref[0, :, g * HEAD_DIM:(g + 1) * HEAD_DIM] = o[g * S:(g + 1) * S].astype(BF16)


def _ctx_attn_call(u3, sink):
    B, S, _ = u3.shape
    gw = GQA * HEAD_DIM
    return pl.pallas_call(
        _ctx_attn_kernel,
        grid=(B, N_KV_HEADS),
        in_specs=[
            pl.BlockSpec(memory_space=pltpu.SMEM),
            pl.BlockSpec((1, S, gw), lambda b, h: (b, 0, Q_OFF // gw + h)),
            pl.BlockSpec((1, S, HEAD_DIM), lambda b, h: (b, 0, K_OFF // HEAD_DIM + h)),
            pl.BlockSpec((1, S, HEAD_DIM), lambda b, h: (b, 0, V_OFF // HEAD_DIM + h)),
        ],
        out_specs=pl.BlockSpec((1, S, gw), lambda b, h: (b, 0, h)),
        out_shape=jax.ShapeDtypeStruct((B, S, ATT_WIDTH), BF16),
        compiler_params=_cparams(("arbitrary", "arbitrary")),
        name="context_attention",
    )(sink, u3, u3, u3)


def _rope(x, cos, sin_signed):
    half = ROPE_AXIS_DIM // 2
    width = x.shape[1]
    lane = lax.broadcasted_iota(jnp.int32, x.shape, 1)
    low_half = (lane & (ROPE_AXIS_DIM - 1)) < half
    partner = jnp.where(low_half, pltpu.roll(x, width - half, axis=1), pltpu.roll(x, half, axis=1))
    return x * cos + partner * sin_signed


def _lat_attn_kernel(sink_ref, q_ref, k_ref, v_ref, ck_ref, cv_ref, cos_ref, sin_ref, o_ref, kr_ref, vb_ref):
    h = pl.program_id(1)
    L = k_ref.shape[1]
    nb = L // BLOCK
    rows4 = GQA * BLOCK
    c2 = SCALE * math.log2(math.e)
    kr_ref[...] = _rope(k_ref[0], cos_ref[:, :HEAD_DIM], sin_ref[:, :HEAD_DIM]).astype(BF16)
    vb_ref[...] = v_ref[0].astype(BF16)
    ck = ck_ref[0, 0].astype(BF16)
    cv = cv_ref[0, 0].astype(BF16)
    nt = (((1,), (1,)), ((), ()))

    r = lax.broadcasted_iota(jnp.int32, (rows4, 1), 0)
    sink = jnp.full((rows4, 1), sink_ref[h * GQA], F32)
    for g in range(1, GQA):
        sink = jnp.where(r >= g * BLOCK, sink_ref[h * GQA + g], sink)
    sink = sink * (1.0 / SCALE)
    a = r & (BLOCK - 1)
    b = lax.broadcasted_iota(jnp.int32, (1, BLOCK), 1)
    keep_prev = b >= a
    keep_next = b <= a

    for n in range(nb):
        rows = slice(n * BLOCK, (n + 1) * BLOCK)
        qa = _rope(q_ref[0, rows, :], cos_ref[rows, :], sin_ref[rows, :]).astype(BF16)
        q = jnp.concatenate([qa[:, g * HEAD_DIM:(g + 1) * HEAD_DIM] for g in range(GQA)], axis=0)
        scores, values = [], []
        for kb, keep in ((n - 1, keep_prev), (n, None), (n + 1, keep_next)):
            if 0 <= kb < nb:
                krows = slice(kb * BLOCK, (kb + 1) * BLOCK)
                s = lax.dot_general(q, kr_ref[krows, :], nt, preferred_element_type=F32)
                scores.append(s if keep is None else jnp.where(keep, s, -1e30))
                values.append(vb_ref[krows, :])
        scores.append(lax.dot_general(q, ck, nt, preferred_element_type=F32))
        values.append(cv)
        m = sink
        for s in scores:
            m = jnp.maximum(m, jnp.max(s, axis=-1, keepdims=True))
        ps = [jnp.exp2((s - m) * c2) for s in scores]
        den = jnp.exp2((sink - m) * c2)
        for p in ps:
            den = den + jnp.sum(p, axis=-1, keepdims=True)
        inv = 1.0 / den
        o = None
        for p, v in zip(ps, values):
            t = _bdot((p * inv).astype(BF16), v)
            o = t if o is None else o + t
        for g in range(GQA):
            o_ref[0, rows, g * HEAD_DIM:(g + 1) * HEAD_DIM] = o[g * BLOCK:(g + 1) * BLOCK].astype(BF16)


def _lat_attn_call(u3, cache_k_l, cache_v_l, layer, sink, cos, sin_signed):
    B, L, _ = u3.shape
    P = cache_k_l.shape[2]
    gw = GQA * HEAD_DIM
    c_spec = pl.BlockSpec((1, 1, P, HEAD_DIM), lambda b, h: (b, layer, 0, h))
    return pl.pallas_call(
        _lat_attn_kernel,
        grid=(B, N_KV_HEADS),
        in_specs=[
            pl.BlockSpec(memory_space=pltpu.SMEM),
            pl.BlockSpec((1, L, gw), lambda b, h: (b, 0, Q_OFF // gw + h)),
            pl.BlockSpec((1, L, HEAD_DIM), lambda b, h: (b, 0, K_OFF // HEAD_DIM + h)),
            pl.BlockSpec((1, L, HEAD_DIM), lambda b, h: (b, 0, V_OFF // HEAD_DIM + h)),
            c_spec, c_spec,
            _const_spec((L, gw)), _const_spec((L, gw)),
        ],
        out_specs=pl.BlockSpec((1, L, gw), lambda b, h: (b, 0, h)),
        out_shape=jax.ShapeDtypeStruct((B, L, ATT_WIDTH), BF16),
        scratch_shapes=[pltpu.VMEM((L, HEAD_DIM), BF16), pltpu.VMEM((L, HEAD_DIM), BF16)],
        compiler_params=_cparams(("arbitrary", "arbitrary")),
        name="latent_attention",
    )(sink, u3, u3, u3, cache_k_l, cache_v_l, jnp.tile(cos, (1, GQA)), jnp.tile(sin_signed, (1, GQA)))


def _rope_tables(L):
    rows = L // GRID_W
    row_pos = jnp.repeat(jnp.arange(rows), GRID_W)
    col_pos = jnp.arange(L) % GRID_W
    half = ROPE_AXIS_DIM // 2
    inv = ROPE_THETA ** (-jnp.arange(half, dtype=F32) * 2.0 / ROPE_AXIS_DIM)
    cos_parts, sin_parts = [], []
    for pos in (row_pos, col_pos):
        ang = pos.astype(F32)[:, None] * inv[None, :]
        cos_parts += [jnp.cos(ang), jnp.cos(ang)]
        sin_parts += [-jnp.sin(ang), jnp.sin(ang)]
    return jnp.concatenate(cos_parts, axis=-1), jnp.concatenate(sin_parts, axis=-1)


def _outproj_kernel(f_ref, hy_ref, att_ref, wf_ref, wh_ref, wa_ref, o_ref):
    o_ref[...] = (_bdot(f_ref[...], wf_ref[...].astype(BF16)) + _bdot(hy_ref[...], wh_ref[...].astype(BF16))
                  + _bdot(att_ref[...], wa_ref[...].astype(BF16)))


def _outproj_call(f, hy, att, w_out, layer):
    T = f.shape[0]
    assert F_WIDTH == HY_WIDTH and ATT_WIDTH == F_WIDTH + HY_WIDTH
    return pl.pallas_call(
        _outproj_kernel,
        grid=(T // TM_MM, D_MODEL // TN_MM),
        in_specs=[
            pl.BlockSpec((TM_MM, F_WIDTH), lambda i, j: (i, 0)),
            pl.BlockSpec((TM_MM, HY_WIDTH), lambda i, j: (i, 0)),
            pl.BlockSpec((TM_MM, ATT_WIDTH), lambda i, j: (i, 0)),
            pl.BlockSpec((None, F_WIDTH, TN_MM), lambda i, j: (layer, 0, j)),
            pl.BlockSpec((None, HY_WIDTH, TN_MM), lambda i, j: (layer, 1, j)),
            pl.BlockSpec((None, ATT_WIDTH, TN_MM), lambda i, j: (layer, 1, j)),
        ],
        out_specs=pl.BlockSpec((TM_MM, TN_MM), lambda i, j: (i, j)),
        out_shape=jax.ShapeDtypeStruct((T, D_MODEL), F32),
        compiler_params=_cparams(("arbitrary", "arbitrary")),
        name="outproj",
    )(f, hy, att, w_out, w_out, w_out)


def _ffn_kernel(seq_len, nf, h_ref, wg_ref, wv_ref, cwg_ref, cwv_ref, cbg_ref, cbv_ref, wd_ref, o_ref,
                u_ref, act_ref):
    g = pl.program_id(0)
    tm = h_ref.shape[0]
    seg = min(seq_len, tm)
    n_seg = tm // seg
    base = [SUBLANE + s * (seg + SUBLANE) for s in range(n_seg)]

    @pl.when(g == 0)
    def _():
        u_ref[...] = jnp.zeros(u_ref.shape, F32)
        act_ref[...] = jnp.zeros(act_ref.shape, BF16)

    @pl.when((g == 0) | (lax.rem(g + (nf - 2), nf) == 0))
    def _():
        o_ref[...] = jnp.zeros(o_ref.shape, F32)

    live = (g > 0).astype(F32)

    def conv_piece(slot, p):
        rp = tm // FFN_CONV_PIECES
        s, r0 = divmod(p * rp, seg)
        lo = base[s] + r0

        def conv(part, cw_ref, cb_ref):
            win = lambda off: u_ref[1 - slot, part, lo + off:lo + off + rp, :]
            return win(-1) * cw_ref[0:1, :] + win(0) * cw_ref[1:2, :] + win(1) * cw_ref[2:3, :] + cb_ref[...]

        gate = conv(0, cwg_ref, cbg_ref)
        val = conv(1, cwv_ref, cbv_ref)
        act_ref[1 - slot, p * rp:(p + 1) * rp, :] = (_silu(gate) * (val * live)).astype(BF16)

    def down_cols(slot, p):
        cp = D_MODEL // FFN_PIECES
        cols = slice(p * cp, (p + 1) * cp)
        o_ref[:, cols] += _bdot(act_ref[slot], wd_ref[:, cols].astype(BF16))

    def up_rows(slot, part, r):
        w_ref = (wg_ref, wv_ref)[part]
        s, r0 = divmod(r * up_rb, seg)
        u_ref[slot, part, base[s] + r0:base[s] + r0 + up_rb, :] = _bdot(
            h_ref[r * up_rb:(r + 1) * up_rb, :], w_ref[...].astype(BF16))

    up_rb = min(seg, FFN_UP_ROWS)
    segments = [(functools.partial(down_cols, p=p), FFN_DOWN_SHARE * tm * (D_MODEL // FFN_PIECES))
                for p in range(FFN_PIECES)]
    segments += [(functools.partial(up_rows, part=part, r=r), up_rb * D_MODEL)
                 for part in (0, 1) for r in range(tm // up_rb)]
    total_time = sum(t for _, t in segments[:-1])

    def step(slot):
        done, emitted = 0, 0
        for run, t in segments:
            while emitted < FFN_CONV_PIECES and emitted * total_time <= done * FFN_CONV_PIECES:
                conv_piece(slot, emitted)
                emitted += 1
            run(slot)
            done += t
        for p in range(emitted, FFN_CONV_PIECES):
            conv_piece(slot, p)

    for slot in (0, 1):
        pl.when((g & 1) == slot)(functools.partial(step, slot))


def _ffn_call(h, seq_len, w_up, w_up_val, conv_w, conv_b, w_down, layer):
    T = h.shape[0]
    nf = D_FF // TF_FFN
    n_tiles = T // TM_FFN
    assert TM_FFN % seq_len == 0
    cb = conv_b.reshape(DEPTH, 1, -1)
    wspec = lambda shape, imap: pl.BlockSpec((None,) + shape, imap)
    seg = min(seq_len, TM_FFN)
    u_rows = SUBLANE + (TM_FFN // seg) * (seg + SUBLANE)
    up_chunk = lambda g: lax.rem(g, nf)
    cv_chunk = lambda g: lax.rem(g + (nf - 1), nf)
    dn_chunk = lambda g: lax.rem(g + (nf - 2), nf)
    up_tile = lambda g: jnp.minimum(g // nf, n_tiles - 1)
    dn_tile = lambda g: jnp.clip((g + (nf - 2)) // nf - 1, 0, n_tiles - 1)
    val_off = nf if w_up_val.shape[-1] == 2 * D_FF else 0
    return pl.pallas_call(
        functools.partial(_ffn_kernel, seq_len, nf),
        grid=(n_tiles * nf + 2,),
        in_specs=[
            pl.BlockSpec((TM_FFN, D_MODEL), lambda g: (up_tile(g), 0), pipeline_mode=pl.Buffered(1)),
            wspec((D_MODEL, TF_FFN), lambda g: (layer, 0, up_chunk(g))),
            wspec((D_MODEL, TF_FFN), lambda g: (layer, 0, val_off + up_chunk(g))),
            wspec((3, TF_FFN), lambda g: (layer, 0, cv_chunk(g))),
            wspec((3, TF_FFN), lambda g: (layer, 0, nf + cv_chunk(g))),
            wspec((1, TF_FFN), lambda g: (layer, 0, cv_chunk(g))),
            wspec((1, TF_FFN), lambda g: (layer, 0, nf + cv_chunk(g))),
            wspec((TF_FFN, D_MODEL), lambda g: (layer, dn_chunk(g), 0)),
        ],
        out_specs=pl.BlockSpec((TM_FFN, D_MODEL), lambda g: (dn_tile(g), 0), pipeline_mode=pl.Buffered(1)),
        out_shape=jax.ShapeDtypeStruct((T, D_MODEL), F32),
        scratch_shapes=[pltpu.VMEM((2, 2, u_rows, TF_FFN), F32), pltpu.VMEM((2, TM_FFN, TF_FFN), BF16)],
        compiler_params=_cparams(("arbitrary",)),
        name="conv_ffn",
    )(h, w_up, w_up_val, conv_w, conv_w, cb, cb, w_down)


def _layer(x, h, B, L, layer, mods, modrow_of_tile, big, p, attend):
    T = B * L
    u3 = _inproj_call(h, big["w_in"], layer).reshape(B, L, IN_WIDTH)
    f = _fourier_call(u3, L)
    spectrum = _filters_call(L, p["flt_w1"], p["flt_b1"], p["flt_w2"], p["flt_b2"], p["flt_w3"], p["flt_freq"])
    hy = _hyena_call(u3, L, p["hy_conv_w"], p["hy_conv_b"], spectrum, p["hy_skip"])
    att = attend(u3)
    proj = _outproj_call(f.reshape(T, F_WIDTH), hy.reshape(T, HY_WIDTH), att.reshape(T, ATT_WIDTH),
                         big["w_out"], layer)
    x, h2 = _resid_ln_call(x, proj, mods, layer, 2, big["ln1_g"], big["ln1_b"], modrow_of_tile,
                           h_layer=layer, h_row=3)
    ff = _ffn_call(h2, L, big["ffn_w_up"], big["ffn_w_up_val"], big["ffn_conv_w"], big["ffn_conv_b"],
                   big["ffn_w_down"], layer)
    if layer + 1 < DEPTH:
        x, h_next = _resid_ln_call(x, ff, mods, layer, 5, big["ln2_g"], big["ln2_b"], modrow_of_tile,
                                   h_layer=layer + 1, h_row=0)
    else:
        x, h_next = _resid_ln_call(x, ff, mods, layer, 5, big["ln2_g"], big["ln2_b"], modrow_of_tile)
    return x, h_next, u3


def kernel(x_prompt, x_sample, cache_k, cache_v, c, c_ctx, w_mod, b_mod, w_in, w_out, attn_sink,
           hy_conv_w, hy_conv_b, flt_w1, flt_b1, flt_w2, flt_b2, flt_w3, flt_freq, hy_skip,
           ln1_g, ln1_b, ffn_w_up, ffn_conv_w, ffn_conv_b, ffn_w_down, ln2_g, ln2_b):
    Bp, S, _ = x_prompt.shape
    Bs, Ls, _ = x_sample.shape
    past = cache_k.shape[2]

    cvec = jnp.zeros((MOD_ROWS, D_MODEL), F32).at[0].set(c_ctx).at[1:1 + Bs].set(c)
    mods = _mod_call(cvec, w_mod, b_mod).reshape(DEPTH, MOD_ROWS, 6, D_MODEL)

    big = dict(w_in=w_in, w_out=w_out, ffn_w_up=ffn_w_up,
               ffn_w_up_val=ffn_w_up,
               ffn_w_down=ffn_w_down, ffn_conv_w=ffn_conv_w, ffn_conv_b=ffn_conv_b,
               ln1_g=ln1_g, ln1_b=ln1_b, ln2_g=ln2_g, ln2_b=ln2_b)
    ck = cache_k.reshape(Bs, DEPTH, past, KV_WIDTH)
    cv = cache_v.reshape(Bs, DEPTH, past, KV_WIDTH)
    cos, sin_signed = _rope_tables(Ls)

    ctx_row = lambda i, tm: 0
    lat_row = lambda i, tm: 1 + i * tm // Ls

    xp = x_prompt.reshape(Bp * S, D_MODEL)
    xs = x_sample.reshape(Bs * Ls, D_MODEL)
    hp = _ln_mod_call(xp, mods, 0, ctx_row)
    hs = _ln_mod_call(xs, mods, 0, lat_row)
    ctx_k, ctx_v = [], []
    for l in range(DEPTH):
        p = dict(flt_w1=flt_w1[l], flt_b1=flt_b1[l], flt_w2=flt_w2[l], flt_b2=flt_b2[l], flt_w3=flt_w3[l],
                 flt_freq=flt_freq[l], hy_conv_w=hy_conv_w[l], hy_conv_b=hy_conv_b[l], hy_skip=hy_skip[l])
        sink = attn_sink[l]
        xp, hp, up3 = _layer(xp, hp, Bp, S, l, mods, ctx_row, big, p, lambda u3: _ctx_attn_call(u3, sink))
        ctx_k.append(up3[:, :, K_OFF:K_OFF + KV_WIDTH].reshape(Bp, S, N_KV_HEADS, HEAD_DIM))
        ctx_v.append(up3[:, :, V_OFF:V_OFF + KV_WIDTH].reshape(Bp, S, N_KV_HEADS, HEAD_DIM))
        xs, hs, _ = _layer(xs, hs, Bs, Ls, l, mods, lat_row, big, p,
                           lambda u3: _lat_attn_call(u3, ck, cv, l, sink, cos, sin_signed))

    return (xp.reshape(Bp, S, D_MODEL), xs.reshape(Bs, Ls, D_MODEL),
            jnp.stack(ctx_k, axis=1), jnp.stack(ctx_v, axis=1))
```
